```python
import jax, jax.numpy as jnp
from jax import lax
import numpy as np

D_MODEL = 2048
BATCH = 4
SEQ = 4096
DEPTH = 1

MIX_WIDTH = D_MODEL
ATTN_WIDTH = MIX_WIDTH // 2
N_ATTN_HEADS = 8
HEAD_DIM = ATTN_WIDTH // N_ATTN_HEADS
ROT_DIM = HEAD_DIM // 4
ROPE_THETA = 500000.0
MOBA_BLOCK = 256
MOBA_TOPK = 3
Q_CHUNK = 32
REC_WIDTH = MIX_WIDTH - ATTN_WIDTH
N_REC_BLOCKS = 16
REC_BLOCK_DIM = REC_WIDTH // N_REC_BLOCKS
CONV_WIDTH = 4
RGLRU_C = 8.0
IN_COLS = 3 * ATTN_WIDTH + 2 * REC_WIDTH
D_FF = ((8 * D_MODEL // 3 + 127) // 128) * 128
FFN_HALF = 0.5
EPS = 1e-6

kernel_name = "macaron_moba_rglru_hybrid_layer"


def rms_norm(x, g):
    x32 = x.astype(jnp.float32)
    y = x32 * lax.rsqrt(jnp.mean(x32 * x32, axis=-1, keepdims=True) + EPS)
    return (y * g.astype(jnp.float32)).astype(x.dtype)


def swiglu(h, w_gate_up, w_down):
    gu = h @ w_gate_up
    g, u = jnp.split(gu, 2, axis=-1)
    return (jax.nn.silu(g) * u) @ w_down


def partial_rope(x, pos):
    inv_freq = ROPE_THETA ** (-jnp.arange(0, ROT_DIM, 2, dtype=jnp.float32) / ROT_DIM)
    ang = pos.astype(jnp.float32)[:, None] * inv_freq[None, :]
    cos, sin = jnp.cos(ang), jnp.sin(ang)
    xr = x[..., :ROT_DIM].astype(jnp.float32)
    x1, x2 = jnp.split(xr, 2, axis=-1)
    rot = jnp.concatenate([x1 * cos - x2 * sin, x2 * cos + x1 * sin], axis=-1)
    return jnp.concatenate([rot.astype(x.dtype), x[..., ROT_DIM:]], axis=-1)


def moba_attention(q, k, v):
    B, H, S, hd = q.shape
    nb = -(-S // MOBA_BLOCK)
    pad = nb * MOBA_BLOCK - S
    kp = jnp.pad(k, ((0, 0), (0, 0), (0, pad), (0, 0)))
    vp = jnp.pad(v, ((0, 0), (0, 0), (0, pad), (0, 0)))
    kb = kp.reshape(B, H, nb, MOBA_BLOCK, hd)
    vb = vp.reshape(B, H, nb, MOBA_BLOCK, hd)
    kmean = jnp.mean(kb.astype(jnp.float32), axis=3)
    k_eff = min(MOBA_TOPK, nb)
    nc = S // Q_CHUNK
    qc = q.reshape(B, H, nc, Q_CHUNK, hd).transpose(2, 0, 1, 3, 4)
    scale = HEAD_DIM ** -0.5
    bi = jnp.arange(B)[:, None, None, None]
    hi = jnp.arange(H)[None, :, None, None]
    blk_ids = jnp.arange(nb)

    def one_chunk(args):
        q_c, c = args
        own = (c * Q_CHUNK) // MOBA_BLOCK
        q_pos = c * Q_CHUNK + jnp.arange(Q_CHUNK)
        q32 = q_c.astype(jnp.float32)
        gate = jnp.einsum('bhqd,bhnd->bhqn', q32, kmean)
        gate = jnp.where(blk_ids < own, gate, -jnp.inf)
        _, sel = lax.top_k(gate, k_eff)
        sel_valid = jnp.arange(k_eff) < own
        k_sel = kb[bi, hi, sel]
        v_sel = vb[bi, hi, sel]
        s_sel = jnp.einsum('bhqd,bhqnkd->bhqnk', q_c, k_sel).astype(jnp.float32) * scale
        s_sel = jnp.where(sel_valid[:, None], s_sel, -jnp.inf)
        s_sel = s_sel.reshape(B, H, Q_CHUNK, k_eff * MOBA_BLOCK)
        k_own = lax.dynamic_index_in_dim(kb, own, axis=2, keepdims=False)
        v_own = lax.dynamic_index_in_dim(vb, own, axis=2, keepdims=False)
        s_own = jnp.einsum('bhqd,bhkd->bhqk', q_c, k_own).astype(jnp.float32) * scale
        key_pos = own * MOBA_BLOCK + jnp.arange(MOBA_BLOCK)
        s_own = jnp.where(key_pos[None, :] <= q_pos[:, None], s_own, -jnp.inf)
        p = jax.nn.softmax(jnp.concatenate([s_sel, s_own], axis=-1), axis=-1)
        p_sel = p[..., :k_eff * MOBA_BLOCK].reshape(B, H, Q_CHUNK, k_eff, MOBA_BLOCK).astype(v.dtype)
        p_own = p[..., k_eff * MOBA_BLOCK:].astype(v.dtype)
        return (jnp.einsum('bhqnk,bhqnkd->bhqd', p_sel, v_sel)
                + jnp.einsum('bhqk,bhkd->bhqd', p_own, v_own))

    out = lax.map(one_chunk, (qc, jnp.arange(nc)))
    return out.transpose(1, 2, 0, 3, 4).reshape(B, H, S, hd)


def causal_depthwise_conv(x, w, b):
    y = lax.conv_general_dilated(
        x, w[:, None, :].astype(x.dtype), window_strides=(1,),
        padding=[(CONV_WIDTH - 1, 0)], dimension_numbers=('NWC', 'WIO', 'NWC'),
        feature_group_count=x.shape[-1])
    return y + b


def rg_lru(x, w_a, b_a, w_x, b_x, lam):
    B, S, R = x.shape
    xb = x.reshape(B, S, N_REC_BLOCKS, REC_BLOCK_DIM)
    r = jax.nn.sigmoid(jnp.einsum('bsni,nij->bsnj', xb, w_a).reshape(B, S, R) + b_a)
    i = jax.nn.sigmoid(jnp.einsum('bsni,nij->bsnj', xb, w_x).reshape(B, S, R) + b_x)
    log_a = (-RGLRU_C * r.astype(jnp.float32)) * jax.nn.softplus(-lam.astype(jnp.float32))
    a = jnp.exp(log_a)
    mult = jnp.sqrt(-jnp.expm1(2.0 * log_a))
    u = mult * (i * x).astype(jnp.float32)

    def combine(c1, c2):
        a1, b1 = c1
        a2, b2 = c2
        return a1 * a2, a2 * b1 + b2

    _, h = lax.associative_scan(combine, (a, u), axis=1)
    return h.astype(x.dtype)


def setup_inputs(seed: int = 0) -> dict:
    key = jax.random.key(seed)
    ks = jax.random.split(key, 32)
    f32 = jnp.float32
    L = DEPTH

    def nrm(k, shape, scale):
        return jax.random.normal(k, shape, f32) * scale

    def gain(k, n):
        return 1.0 + 0.05 * jax.random.normal(k, (L, n), f32)

    a0 = jax.random.uniform(ks[13], (L, REC_WIDTH), f32, 0.9, 0.999)
    return {
        "x": nrm(ks[0], (BATCH, SEQ, D_MODEL), 1.0),
        "ffn1_pre_g": gain(ks[1], D_MODEL),
        "ffn1_w_gate_up": nrm(ks[2], (L, D_MODEL, 2 * D_FF), D_MODEL ** -0.5),
        "ffn1_w_down": nrm(ks[3], (L, D_FF, D_MODEL), D_FF ** -0.5),
        "ffn1_post_g": gain(ks[4], D_MODEL),
        "mix_pre_g": gain(ks[5], D_MODEL),
        "w_in": nrm(ks[6], (L, D_MODEL, IN_COLS), D_MODEL ** -0.5),
        "conv_w": nrm(ks[7], (L, CONV_WIDTH, REC_WIDTH), CONV_WIDTH ** -0.5),
        "conv_b": nrm(ks[8], (L, REC_WIDTH), 0.01),
        "rg_w_a": nrm(ks[9], (L, N_REC_BLOCKS, REC_BLOCK_DIM, REC_BLOCK_DIM), REC_BLOCK_DIM ** -0.5),
        "rg_b_a": nrm(ks[10], (L, REC_WIDTH), 0.01),
        "rg_w_x": nrm(ks[11], (L, N_REC_BLOCKS, REC_BLOCK_DIM, REC_BLOCK_DIM), REC_BLOCK_DIM ** -0.5),
        "rg_b_x": nrm(ks[12], (L, REC_WIDTH), 0.01),
        "rg_lambda": jnp.log(a0) - jnp.log1p(-a0),
        "attn_out_g": gain(ks[14], ATTN_WIDTH),
        "rec_out_g": gain(ks[15], REC_WIDTH),
        "w_out": nrm(ks[16], (L, MIX_WIDTH, D_MODEL), MIX_WIDTH ** -0.5),
        "mix_post_g": gain(ks[17], D_MODEL),
        "ffn2_pre_g": gain(ks[18], D_MODEL),
        "ffn2_w_gate_up": nrm(ks[19], (L, D_MODEL, 2 * D_FF), D_MODEL ** -0.5),
        "ffn2_w_down": nrm(ks[20], (L, D_FF, D_MODEL), D_FF ** -0.5),
        "ffn2_post_g": gain(ks[21], D_MODEL),
    }


def reference(x, ffn1_pre_g, ffn1_w_gate_up, ffn1_w_down, ffn1_post_g,
              mix_pre_g, w_in, conv_w, conv_b, rg_w_a, rg_b_a, rg_w_x, rg_b_x,
              rg_lambda, attn_out_g, rec_out_g, w_out, mix_post_g,
              ffn2_pre_g, ffn2_w_gate_up, ffn2_w_down, ffn2_post_g):
    B, S, _ = x.shape
    pos = jnp.arange(S, dtype=jnp.int32)
    splits = [ATTN_WIDTH, 2 * ATTN_WIDTH, 3 * ATTN_WIDTH, 3 * ATTN_WIDTH + REC_WIDTH]

    def heads(t):
        return t.reshape(B, S, N_ATTN_HEADS, HEAD_DIM).transpose(0, 2, 1, 3)

    for l in range(DEPTH):
        f = swiglu(rms_norm(x, ffn1_pre_g[l]), ffn1_w_gate_up[l], ffn1_w_down[l])
        x = x + FFN_HALF * rms_norm(f, ffn1_post_g[l])

        h = rms_norm(x, mix_pre_g[l])
        proj = h @ w_in[l]
        q, k, v, xr, gr = jnp.split(proj, splits, axis=-1)
        q = partial_rope(heads(q), pos)
        k = partial_rope(heads(k), pos)
        attn = moba_attention(q, k, heads(v))
        attn = attn.transpose(0, 2, 1, 3).reshape(B, S, ATTN_WIDTH)
        xc = causal_depthwise_conv(xr, conv_w[l], conv_b[l])
        rec = rg_lru(xc, rg_w_a[l], rg_b_a[l], rg_w_x[l], rg_b_x[l], rg_lambda[l])
        rec = rec * jax.nn.gelu(gr)
        merged = jnp.concatenate([rms_norm(attn, attn_out_g[l]),
                                  rms_norm(rec, rec_out_g[l])], axis=-1)
        y = merged @ w_out[l]
        x = x + rms_norm(y, mix_post_g[l])

        f = swiglu(rms_norm(x, ffn2_pre_g[l]), ffn2_w_gate_up[l], ffn2_w_down[l])
        x = x + FFN_HALF * rms_norm(f, ffn2_post_g[l])
    return x
```

```python
import functools
import math

import jax
import jax.numpy as jnp
from jax import lax
from jax.experimental import pallas as pl
from jax.experimental.pallas import tpu as pltpu

F32 = jnp.float32
BF16 = jnp.bfloat16

N_ATTN_HEADS = 8
HEAD_DIM = 128
ROT_DIM = 32
ROPE_THETA = 500000.0
MOBA_BLOCK = 256
MOBA_TOPK = 3
N_REC_BLOCKS = 16
CONV_WIDTH = 4
RGLRU_C = 8.0
FFN_HALF = 0.5
EPS = 1e-6

V7X_VMEM_BYTES = 64 * 1024 * 1024
VMEM_LIMIT = 56 * 1024 * 1024
LANES = 128
SUBLANES = 8

NEG_BIG = -1e30
LOG2E = 1.4426950408889634


def _rms(x, g):
    ms = jnp.mean(x * x, axis=-1, keepdims=True)
    return x * lax.rsqrt(ms + EPS) * g


def _ffn_kernel(x_ref, preg_ref, wgu_ref, wd_ref, postg_ref, o_ref, h_scr, acc_scr, *, tf):
    j = pl.program_id(1)

    @pl.when(j == 0)
    def _():
        h_scr[...] = _rms(x_ref[...], preg_ref[...]).astype(BF16)
        acc_scr[...] = jnp.zeros_like(acc_scr)

    gu = jnp.dot(h_scr[...], wgu_ref[0], preferred_element_type=F32)
    g = gu[:, :tf]
    u = gu[:, tf:]
    a = (g * jax.nn.sigmoid(g) * u).astype(BF16)
    acc_scr[...] += jnp.dot(a, wd_ref[0], preferred_element_type=F32)

    @pl.when(j == pl.num_programs(1) - 1)
    def _():
        o_ref[...] = x_ref[...] + FFN_HALF * _rms(acc_scr[...], postg_ref[...])


def _prep_ffn_weights(w_gate_up, w_down, tf):
    d, two_f = w_gate_up.shape
    f = two_f // 2
    n = -(-f // tf)
    fp = n * tf
    wg = jnp.pad(w_gate_up[:, :f].astype(BF16), ((0, 0), (0, fp - f))).reshape(d, n, tf)
    wu = jnp.pad(w_gate_up[:, f:].astype(BF16), ((0, 0), (0, fp - f))).reshape(d, n, tf)
    wgu = jnp.concatenate([wg, wu], axis=2).transpose(1, 0, 2)
    wd = jnp.pad(w_down.astype(BF16), ((0, fp - f), (0, 0))).reshape(n, tf, d)
    return wgu, wd


def _ffn(x2d, pre_g, w_gate_up, w_down, post_g, *, tm=512, tf=512):
    t, d = x2d.shape
    wgu, wd = _prep_ffn_weights(w_gate_up, w_down, tf)
    n = wgu.shape[0]
    return pl.pallas_call(
        functools.partial(_ffn_kernel, tf=tf),
        grid=(t // tm, n),
        in_specs=[
            pl.BlockSpec((tm, d), lambda i, j: (i, 0)),
            pl.BlockSpec((1, d), lambda i, j: (0, 0)),
            pl.BlockSpec((1, d, 2 * tf), lambda i, j: (j, 0, 0)),
            pl.BlockSpec((1, tf, d), lambda i, j: (j, 0, 0)),
            pl.BlockSpec((1, d), lambda i, j: (0, 0)),
        ],
        out_specs=pl.BlockSpec((tm, d), lambda i, j: (i, 0)),
        out_shape=jax.ShapeDtypeStruct((t, d), F32),
        scratch_shapes=[pltpu.VMEM((tm, d), BF16), pltpu.VMEM((tm, d), F32)],
        compiler_params=pltpu.CompilerParams(
            dimension_semantics=("parallel", "arbitrary"), vmem_limit_bytes=VMEM_LIMIT),
        name="ffn",
    )(x2d, pre_g.reshape(1, d), wgu, wd, post_g.reshape(1, d))


def _in_proj_kernel(x_ref, g_ref, w_ref, cos_ref, sin_ref, qkv_ref, rec_ref, h_scr,
                    *, n_rope_tiles, n_qkv_tiles, heads_per_tile):
    j = pl.program_id(1)

    @pl.when(j == 0)
    def _():
        h_scr[...] = _rms(x_ref[...], g_ref[...]).astype(BF16)

    y = jnp.dot(h_scr[...], w_ref[...], preferred_element_type=F32)

    @pl.when(j < n_rope_tiles)
    def _():
        c = cos_ref[...]
        s = sin_ref[...]
        lane = lax.broadcasted_iota(jnp.int32, c.shape, 1)
        first_half = lane < ROT_DIM // 2
        for hh in range(heads_per_tile):
            yh = y[:, hh * HEAD_DIM:(hh + 1) * HEAD_DIM]
            partner = jnp.where(first_half,
                                pltpu.roll(yh, HEAD_DIM - ROT_DIM // 2, 1),
                                pltpu.roll(yh, ROT_DIM // 2, 1))
            qkv_ref[:, hh * HEAD_DIM:(hh + 1) * HEAD_DIM] = (yh * c + partner * s).astype(BF16)

    @pl.when(jnp.logical_and(j >= n_rope_tiles, j < n_qkv_tiles))
    def _():
        qkv_ref[...] = y.astype(BF16)

    @pl.when(j >= n_qkv_tiles)
    def _():
        rec_ref[...] = y


def _rope_tables(seq):
    inv_freq = ROPE_THETA ** (-jnp.arange(0, ROT_DIM, 2, dtype=F32) / ROT_DIM)
    ang = jnp.arange(seq, dtype=jnp.int32).astype(F32)[:, None] * inv_freq[None, :]
    cos, sin = jnp.cos(ang), jnp.sin(ang)
    ones = jnp.ones((seq, HEAD_DIM - ROT_DIM), F32)
    zeros = jnp.zeros((seq, HEAD_DIM - ROT_DIM), F32)
    cos_t = jnp.concatenate([cos, cos, ones], axis=1)
    sin_t = jnp.concatenate([-sin, sin, zeros], axis=1)
    return cos_t, sin_t


def _in_proj(x2d, g, w_in, seq, attn_width, rec_width, *, tm=512, tn=512):
    t, d = x2d.shape
    cols = w_in.shape[1]
    n_tiles = cols // tn
    n_rope_tiles = 2 * attn_width // tn
    n_qkv_tiles = 3 * attn_width // tn
    n_rec_tiles = n_tiles - n_qkv_tiles
    s_tiles = seq // tm
    cos_t, sin_t = _rope_tables(seq)
    kern = functools.partial(_in_proj_kernel, n_rope_tiles=n_rope_tiles, n_qkv_tiles=n_qkv_tiles,
                             heads_per_tile=tn // HEAD_DIM)
    return pl.pallas_call(
        kern,
        grid=(t // tm, n_tiles),
        in_specs=[
            pl.BlockSpec((tm, d), lambda i, j: (i, 0)),
            pl.BlockSpec((1, d), lambda i, j: (0, 0)),
            pl.BlockSpec((d, tn), lambda i, j: (0, j)),
            pl.BlockSpec((tm, HEAD_DIM), lambda i, j: (i % s_tiles, 0)),
            pl.BlockSpec((tm, HEAD_DIM), lambda i, j: (i % s_tiles, 0)),
        ],
        out_specs=[
            pl.BlockSpec((tm, tn), lambda i, j: (i, jnp.minimum(j, n_qkv_tiles - 1))),
            pl.BlockSpec((tm, tn), lambda i, j: (i, jnp.maximum(j - n_qkv_tiles, 0))),
        ],
        out_shape=[
            jax.ShapeDtypeStruct((t, 3 * attn_width), BF16),
            jax.ShapeDtypeStruct((t, n_rec_tiles * tn), F32),
        ],
        scratch_shapes=[pltpu.VMEM((tm, d), BF16)],
        compiler_params=pltpu.CompilerParams(
            dimension_semantics=("parallel", "arbitrary"), vmem_limit_bytes=VMEM_LIMIT),
        name="in_proj",
    )(x2d, g.reshape(1, d), w_in.astype(BF16), cos_t, sin_t)


def _moba_kernel(q_ref, k_ref, v_ref, o_ref, vt_scr, kmean_scr, bias_scr, *, nb):
    blk = MOBA_BLOCK
    scale_l2 = (HEAD_DIM ** -0.5) * LOG2E
    tb = (((1,), (1,)), ((), ()))

    for n in range(nb):
        rows = slice(n * blk, (n + 1) * blk)
        vt_scr[n] = v_ref[rows, :].astype(F32).T.astype(BF16)
        kmean_scr[n:n + 1, :] = jnp.mean(k_ref[rows, :].astype(F32), axis=0, keepdims=True)

    km = kmean_scr[...]
    km_hi = km.astype(BF16)
    km_lo = (km - km_hi.astype(F32)).astype(BF16)
    blk_id = lax.broadcasted_iota(jnp.int32, (nb, blk), 0)
    key_pos = lax.broadcasted_iota(jnp.int32, (blk, blk), 0)
    qry_pos = lax.broadcasted_iota(jnp.int32, (blk, blk), 1)
    causal = key_pos <= qry_pos

    def q_block(qi, _):
        q0 = pl.multiple_of(qi * blk, blk)
        q = q_ref[pl.ds(q0, blk), :]

        gate = (lax.dot_general(km_hi, q, tb, preferred_element_type=F32)
                + lax.dot_general(km_lo, q, tb, preferred_element_type=F32))
        eligible = blk_id < qi
        rem = jnp.where(eligible, gate, -jnp.inf)
        sel = jnp.zeros((nb, blk), jnp.bool_)
        for _t in range(MOBA_TOPK):
            mx = jnp.max(rem, axis=0, keepdims=True)
            first = jnp.min(jnp.where(rem == mx, blk_id, nb), axis=0, keepdims=True)
            pick = blk_id == first
            sel = jnp.logical_or(sel, pick)
            rem = jnp.where(pick, -jnp.inf, rem)
        sel = jnp.logical_and(sel, eligible)
        bias_scr[...] = jnp.where(sel, 0.0, NEG_BIG)

        kd = k_ref[pl.ds(q0, blk), :]
        st = lax.dot_general(kd, q, tb, preferred_element_type=F32) * scale_l2
        st = jnp.where(causal, st, NEG_BIG)
        m = jnp.max(st, axis=0, keepdims=True)
        p = jnp.exp2(st - m)
        l = jnp.sum(p, axis=0, keepdims=True)
        acc = jnp.dot(vt_scr[qi], p.astype(BF16), preferred_element_type=F32)

        def kv_block(n, carry):
            m, l, acc = carry
            n0 = pl.multiple_of(n * blk, blk)
            kn = k_ref[pl.ds(n0, blk), :]
            raw = lax.dot_general(kn, q, tb, preferred_element_type=F32)
            bias = bias_scr[pl.ds(n, 1), :]
            cm = jnp.max(raw, axis=0, keepdims=True) * scale_l2 + bias
            m_new = jnp.maximum(m, cm)
            alpha = jnp.exp2(m - m_new)
            p = jnp.exp2(raw * scale_l2 + (bias - m_new))
            l = l * alpha + jnp.sum(p, axis=0, keepdims=True)
            acc = acc * alpha + jnp.dot(vt_scr[n], p.astype(BF16), preferred_element_type=F32)
            return m_new, l, acc

        m, l, acc = lax.fori_loop(0, qi, kv_block, (m, l, acc))
        o_ref[pl.ds(q0, blk), :] = (acc / l).T
        return 0

    lax.fori_loop(0, nb, q_block, 0)


def _moba(qkv, batch, seq):
    t = qkv.shape[0]
    nb = seq // MOBA_BLOCK
    nh = N_ATTN_HEADS
    return pl.pallas_call(
        functools.partial(_moba_kernel, nb=nb),
        grid=(batch, nh),
        in_specs=[
            pl.BlockSpec((seq, HEAD_DIM), lambda b, h: (b, h)),
            pl.BlockSpec((seq, HEAD_DIM), lambda b, h: (b, nh + h)),
            pl.BlockSpec((seq, HEAD_DIM), lambda b, h: (b, 2 * nh + h)),
        ],
        out_specs=pl.BlockSpec((seq, HEAD_DIM), lambda b, h: (b, h)),
        out_shape=jax.ShapeDtypeStruct((t, nh * HEAD_DIM), F32),
        scratch_shapes=[
            pltpu.VMEM((nb, HEAD_DIM, MOBA_BLOCK), BF16),
            pltpu.VMEM((nb, HEAD_DIM), F32),
            pltpu.VMEM((nb, MOBA_BLOCK), F32),
        ],
        compiler_params=pltpu.CompilerParams(
            dimension_semantics=("parallel", "parallel"), vmem_limit_bytes=VMEM_LIMIT),
        name="moba",
    )(qkv, qkv, qkv)


def _rglru_kernel(xr_ref, gr_ref, cw_ref, cb_ref, wg_ref, ba_ref, bx_ref, lam_ref, o_ref,
                  xbuf, hcar, a_scr, b_scr, *, tl, cw):
    t = pl.program_id(2)
    pad = SUBLANES

    @pl.when(t == 0)
    def _():
        xbuf[0:pad, :] = jnp.zeros((pad, cw), F32)
        hcar[...] = jnp.zeros_like(hcar)

    xbuf[pad:pad + tl, :] = xr_ref[...]
    w = cw_ref[...]
    xc = cb_ref[...] + w[CONV_WIDTH - 1:CONV_WIDTH] * xbuf[pad:pad + tl, :]
    for j in range(1, CONV_WIDTH):
        xc = xc + w[CONV_WIDTH - 1 - j:CONV_WIDTH - j] * xbuf[pad - j:pad - j + tl, :]
    xbuf[0:pad, :] = xbuf[tl:tl + pad, :]

    gates = jnp.dot(xc.astype(BF16), wg_ref[0], preferred_element_type=F32)
    r = jax.nn.sigmoid(gates[:, :cw] + ba_ref[...])
    i = jax.nn.sigmoid(gates[:, cw:] + bx_ref[...])
    z = -lam_ref[...]
    softplus = jnp.maximum(z, 0.0) + jnp.log1p(jnp.exp(-jnp.abs(z)))
    log_a = (-RGLRU_C * r) * softplus
    a = jnp.exp(log_a)
    b = jnp.sqrt(-jnp.tanh(log_a) * (a * a + 1.0)) * (i * xc)

    row8 = lax.broadcasted_iota(jnp.int32, (tl, cw), 0) & (SUBLANES - 1)
    for s in (1, 2, 4):
        ar = pltpu.roll(a, s, 0)
        br = pltpu.roll(b, s, 0)
        keep = row8 >= s
        b = jnp.where(keep, a * br + b, b)
        a = jnp.where(keep, a * ar, a)
    a_scr[...] = a
    b_scr[...] = b

    def group(gidx, carry):
        r0 = pl.multiple_of(gidx * SUBLANES, SUBLANES)
        hg = a_scr[pl.ds(r0, SUBLANES), :] * carry + b_scr[pl.ds(r0, SUBLANES), :]
        b_scr[pl.ds(r0, SUBLANES), :] = hg
        return hg[SUBLANES - 1:SUBLANES, :]

    hcar[...] = lax.fori_loop(0, tl // SUBLANES, group, hcar[...], unroll=8)
    o_ref[...] = b_scr[...] * jax.nn.gelu(gr_ref[...])


def _block_diag_gates(w_a, w_x, cw):
    nblk, bd, _ = w_a.shape
    per = cw // bd
    nt = nblk // per
    eye = jnp.eye(per, dtype=w_a.dtype)

    def bdiag(w):
        w = w.reshape(nt, per, bd, bd)
        full = jnp.einsum("tpij,pq->tpiqj", w, eye)
        return full.reshape(nt, cw, cw)

    return jnp.concatenate([bdiag(w_a), bdiag(w_x)], axis=2).astype(BF16)


def _rglru(rec_in, conv_w, conv_b, w_a, b_a, w_x, b_x, lam, batch, seq, *, tl=512, cw=256):
    t = rec_in.shape[0]
    r_width = conv_w.shape[1]
    nc = r_width // cw
    nt = seq // tl
    wg = _block_diag_gates(w_a, w_x, cw)
    row = lambda v: v.reshape(1, r_width)
    vec_spec = pl.BlockSpec((1, cw), lambda b, c, s: (0, c))
    return pl.pallas_call(
        functools.partial(_rglru_kernel, tl=tl, cw=cw),
        grid=(batch, nc, nt),
        in_specs=[
            pl.BlockSpec((tl, cw), lambda b, c, s: (b * nt + s, c)),
            pl.BlockSpec((tl, cw), lambda b, c, s: (b * nt + s, nc + c)),
            pl.BlockSpec((CONV_WIDTH, cw), lambda b, c, s: (0, c)),
            vec_spec,
            pl.BlockSpec((1, cw, 2 * cw), lambda b, c, s: (c, 0, 0)),
            vec_spec, vec_spec, vec_spec,
        ],
        out_specs=pl.BlockSpec((tl, cw), lambda b, c, s: (b * nt + s, c)),
        out_shape=jax.ShapeDtypeStruct((t, r_width), F32),
        scratch_shapes=[
            pltpu.VMEM((tl + SUBLANES, cw), F32),
            pltpu.VMEM((1, cw), F32),
            pltpu.VMEM((tl, cw), F32),
            pltpu.VMEM((tl, cw), F32),
        ],
        compiler_params=pltpu.CompilerParams(
            dimension_semantics=("parallel", "parallel", "arbitrary"), vmem_limit_bytes=VMEM_LIMIT),
        name="rglru",
    )(rec_in, rec_in, conv_w, row(conv_b), wg, row(b_a), row(b_x), row(lam))


def _out_proj_kernel(x_ref, attn_ref, rec_ref, ga_ref, gr_ref, w_ref, pg_ref, o_ref):
    a = _rms(attn_ref[...], ga_ref[...]).astype(BF16)
    r = _rms(rec_ref[...], gr_ref[...]).astype(BF16)
    aw = a.shape[1]
    y = (jnp.dot(a, w_ref[:aw, :], preferred_element_type=F32)
         + jnp.dot(r, w_ref[aw:, :], preferred_element_type=F32))
    o_ref[...] = x_ref[...] + _rms(y, pg_ref[...])


def _out_proj(x2d, attn, rec, ga, gr, w_out, pg, *, tm=512):
    t, d = x2d.shape
    aw = attn.shape[1]
    rw = rec.shape[1]
    return pl.pallas_call(
        _out_proj_kernel,
        grid=(t // tm,),
        in_specs=[
            pl.BlockSpec((tm, d), lambda i: (i, 0)),
            pl.BlockSpec((tm, aw), lambda i: (i, 0)),
            pl.BlockSpec((tm, rw), lambda i: (i, 0)),
            pl.BlockSpec((1, aw), lambda i: (0, 0)),
            pl.BlockSpec((1, rw), lambda i: (0, 0)),
            pl.BlockSpec((aw + rw, d), lambda i: (0, 0)),
            pl.BlockSpec((1, d), lambda i: (0, 0)),
        ],
        out_specs=pl.BlockSpec((tm, d), lambda i: (i, 0)),
        out_shape=jax.ShapeDtypeStruct((t, d), F32),
        compiler_params=pltpu.CompilerParams(
            dimension_semantics=("parallel",), vmem_limit_bytes=VMEM_LIMIT),
        name="out_proj",
    )(x2d, attn, rec, ga.reshape(1, aw), gr.reshape(1, rw), w_out.astype(BF16), pg.reshape(1, d))


def kernel(x, ffn1_pre_g, ffn1_w_gate_up, ffn1_w_down, ffn1_post_g, mix_pre_g, w_in, conv_w, conv_b, rg_w_a, rg_b_a, rg_w_x, rg_b_x, rg_lambda, attn_out_g, rec_out_g, w_out, mix_post_g, ffn2_pre_g, ffn2_w_gate_up, ffn2_w_down, ffn2_post_g):
    batch, seq, d = x.shape
    depth = ffn1_pre_g.shape[0]
    attn_width = N_ATTN_HEADS * HEAD_DIM
    rec_width = conv_w.shape[-1]
    assert w_in.shape[-1] == 3 * attn_width + 2 * rec_width
    assert seq % 512 == 0 and seq % MOBA_BLOCK == 0
    h = x.reshape(batch * seq, d)
    for l in range(depth):
        h = _ffn(h, ffn1_pre_g[l], ffn1_w_gate_up[l], ffn1_w_down[l], ffn1_post_g[l])
        qkv, rec_in = _in_proj(h, mix_pre_g[l], w_in[l], seq, attn_width, rec_width)
        attn = _moba(qkv, batch, seq)
        rec = _rglru(rec_in, conv_w[l], conv_b[l], rg_w_a[l], rg_b_a[l], rg_w_x[l], rg_b_x[l],
                     rg_lambda[l], batch, seq)
        h = _out_proj(h, attn, rec, attn_out_g[l], rec_out_g[l], w_out[l], mix_post_g[l])
        h = _ffn(h, ffn2_pre_g[l], ffn2_w_gate_up[l], ffn2_w_down[l], ffn2_post_g[l])
    return h.reshape(batch, seq, d)
```

```python
import functools
import math

import jax
import jax.numpy as jnp
from jax import lax
from jax.experimental import pallas as pl
from jax.experimental.pallas import tpu as pltpu

F32 = jnp.float32
BF16 = jnp.bfloat16

N_ATTN_HEADS = 8
HEAD_DIM = 128
ROT_DIM = 32
ROPE_THETA = 500000.0
MOBA_BLOCK = 256
MOBA_TOPK = 3
N_REC_BLOCKS = 16
CONV_WIDTH = 4
RGLRU_C = 8.0
FFN_HALF = 0.5
EPS = 1e-6

V7X_VMEM_BYTES = 64 * 1024 * 1024
VMEM_LIMIT = 56 * 1024 * 1024
LANES = 128
SUBLANES = 8

NEG_BIG = -1e30
LOG2E = 1.4426950408889634


def _rms(x, g):
    ms = jnp.mean(x * x, axis=-1, keepdims=True)
    return x * lax.rsqrt(ms + EPS) * g


def _ffn_kernel(x_ref, preg_ref, wg_ref, wu_ref, wd_ref, postg_ref, o_ref, h_scr, acc_scr):
    j = pl.program_id(1)

    @pl.when(j == 0)
    def _():
        h_scr[...] = _rms(x_ref[...], preg_ref[...]).astype(BF16)
        acc_scr[...] = jnp.zeros_like(acc_scr)

    h = h_scr[...]
    g = jnp.dot(h, wg_ref[...], preferred_element_type=F32)
    u = jnp.dot(h, wu_ref[...], preferred_element_type=F32)
    a = (g * jax.nn.sigmoid(g) * u).astype(BF16)
    acc_scr[...] += jnp.dot(a, wd_ref[...], preferred_element_type=F32)

    @pl.when(j == pl.num_programs(1) - 1)
    def _():
        o_ref[...] = x_ref[...] + FFN_HALF * _rms(acc_scr[...], postg_ref[...])


def _prep_ffn_weights(w_gate_up, w_down, tf):
    f = w_down.shape[0]
    fp = -(-f // tf) * tf
    wg = jnp.pad(w_gate_up[:, :f].astype(BF16), ((0, 0), (0, fp - f)))
    wu = jnp.pad(w_gate_up[:, f:].astype(BF16), ((0, 0), (0, fp - f)))
    wd = jnp.pad(w_down.astype(BF16), ((0, fp - f), (0, 0)))
    return wg, wu, wd


def _ffn(x2d, pre_g, w_gate_up, w_down, post_g, *, tm=512, tf=512):
    t, d = x2d.shape
    wg, wu, wd = _prep_ffn_weights(w_gate_up, w_down, tf)
    n = wd.shape[0] // tf
    return pl.pallas_call(
        _ffn_kernel,
        grid=(t // tm, n),
        in_specs=[
            pl.BlockSpec((tm, d), lambda i, j: (i, 0)),
            pl.BlockSpec((1, d), lambda i, j: (0, 0)),
            pl.BlockSpec((d, tf), lambda i, j: (0, j)),
            pl.BlockSpec((d, tf), lambda i, j: (0, j)),
            pl.BlockSpec((tf, d), lambda i, j: (j, 0)),
            pl.BlockSpec((1, d), lambda i, j: (0, 0)),
        ],
        out_specs=pl.BlockSpec((tm, d), lambda i, j: (i, 0)),
        out_shape=jax.ShapeDtypeStruct((t, d), F32),
        scratch_shapes=[pltpu.VMEM((tm, d), BF16), pltpu.VMEM((tm, d), F32)],
        compiler_params=pltpu.CompilerParams(
            dimension_semantics=("parallel", "arbitrary"), vmem_limit_bytes=VMEM_LIMIT),
        name="ffn",
    )(x2d, pre_g.reshape(1, d), wg, wu, wd, post_g.reshape(1, d))


def _in_proj_kernel(x_ref, g_ref, w_ref, cos_ref, sin_ref, qkv_ref, rec_ref,
                    *, tn, n_rope_tiles, n_qkv_tiles, n_tiles):
    h = _rms(x_ref[...], g_ref[...]).astype(BF16)
    c = cos_ref[...]
    s = sin_ref[...]
    lane = lax.broadcasted_iota(jnp.int32, c.shape, 1)
    first_half = lane < ROT_DIM // 2
    for j in range(n_tiles):
        y = jnp.dot(h, w_ref[:, j * tn:(j + 1) * tn], preferred_element_type=F32)
        if j < n_rope_tiles:
            for hh in range(tn // HEAD_DIM):
                yh = y[:, hh * HEAD_DIM:(hh + 1) * HEAD_DIM]
                partner = jnp.where(first_half,
                                    pltpu.roll(yh, HEAD_DIM - ROT_DIM // 2, 1),
                                    pltpu.roll(yh, ROT_DIM // 2, 1))
                col = j * tn + hh * HEAD_DIM
                qkv_ref[:, col:col + HEAD_DIM] = (yh * c + partner * s).astype(BF16)
        elif j < n_qkv_tiles:
            qkv_ref[:, j * tn:(j + 1) * tn] = y.astype(BF16)
        else:
            rec_ref[:, (j - n_qkv_tiles) * tn:(j - n_qkv_tiles + 1) * tn] = y


def _rope_tables(seq):
    inv_freq = ROPE_THETA ** (-jnp.arange(0, ROT_DIM, 2, dtype=F32) / ROT_DIM)
    ang = jnp.arange(seq, dtype=jnp.int32).astype(F32)[:, None] * inv_freq[None, :]
    cos, sin = jnp.cos(ang), jnp.sin(ang)
    ones = jnp.ones((seq, HEAD_DIM - ROT_DIM), F32)
    zeros = jnp.zeros((seq, HEAD_DIM - ROT_DIM), F32)
    cos_t = jnp.concatenate([cos, cos, ones], axis=1)
    sin_t = jnp.concatenate([-sin, sin, zeros], axis=1)
    return cos_t, sin_t


def _in_proj(x2d, g, w_in, seq, attn_width, rec_width, *, tm=256, tn=512):
    t, d = x2d.shape
    cols = w_in.shape[1]
    n_tiles = cols // tn
    n_rope_tiles = 2 * attn_width // tn
    n_qkv_tiles = 3 * attn_width // tn
    s_tiles = seq // tm
    cos_t, sin_t = _rope_tables(seq)
    kern = functools.partial(_in_proj_kernel, tn=tn, n_rope_tiles=n_rope_tiles,
                             n_qkv_tiles=n_qkv_tiles, n_tiles=n_tiles)
    return pl.pallas_call(
        kern,
        grid=(t // tm,),
        in_specs=[
            pl.BlockSpec((tm, d), lambda i: (i, 0)),
            pl.BlockSpec((1, d), lambda i: (0, 0)),
            pl.BlockSpec((d, cols), lambda i: (0, 0), pipeline_mode=pl.Buffered(1)),
            pl.BlockSpec((tm, HEAD_DIM), lambda i: (i % s_tiles, 0)),
            pl.BlockSpec((tm, HEAD_DIM), lambda i: (i % s_tiles, 0)),
        ],
        out_specs=[
            pl.BlockSpec((tm, 3 * attn_width), lambda i: (i, 0)),
            pl.BlockSpec((tm, 2 * rec_width), lambda i: (i, 0)),
        ],
        out_shape=[
            jax.ShapeDtypeStruct((t, 3 * attn_width), BF16),
            jax.ShapeDtypeStruct((t, 2 * rec_width), F32),
        ],
        compiler_params=pltpu.CompilerParams(
            dimension_semantics=("parallel",), vmem_limit_bytes=VMEM_LIMIT),
        name="in_proj",
    )(x2d, g.reshape(1, d), w_in.astype(BF16), cos_t, sin_t)


def _moba_kernel(q_ref, k_ref, v_ref, o_ref, vt_scr, kmean_scr, s_scr, p_scr, *, nb):
    blk = MOBA_BLOCK
    scale_l2 = (HEAD_DIM ** -0.5) * LOG2E
    tb = (((1,), (1,)), ((), ()))

    for n in range(nb):
        rows = slice(n * blk, (n + 1) * blk)
        vt_scr[:, rows] = v_ref[rows, :].astype(F32).T.astype(BF16)
        kmean_scr[n:n + 1, :] = jnp.mean(k_ref[rows, :].astype(F32), axis=0, keepdims=True)

    km = kmean_scr[...]
    km_hi = km.astype(BF16)
    km_lo = (km - km_hi.astype(F32)).astype(BF16)
    blk_id = lax.broadcasted_iota(jnp.int32, (nb, blk), 0)
    key_pos = lax.broadcasted_iota(jnp.int32, (blk, blk), 0)
    qry_pos = lax.broadcasted_iota(jnp.int32, (blk, blk), 1)
    causal = key_pos <= qry_pos

    for qi in range(nb):
        slot = qi % 2
        nk = (qi + 1) * blk
        q = q_ref[qi * blk:(qi + 1) * blk, :]

        if qi <= MOBA_TOPK:
            bias = None
        else:
            gate = (lax.dot_general(km_hi, q, tb, preferred_element_type=F32)
                    + lax.dot_general(km_lo, q, tb, preferred_element_type=F32))
            eligible = blk_id < qi
            rem = jnp.where(eligible, gate, -jnp.inf)
            sel = jnp.zeros((nb, blk), jnp.bool_)
            for _t in range(MOBA_TOPK):
                mx = jnp.max(rem, axis=0, keepdims=True)
                first = jnp.min(jnp.where(rem == mx, blk_id, nb), axis=0, keepdims=True)
                pick = blk_id == first
                sel = jnp.logical_or(sel, pick)
                rem = jnp.where(pick, -jnp.inf, rem)
            bias = jnp.where(jnp.logical_and(sel, eligible), 0.0, NEG_BIG)

        m = None
        for n in range(qi + 1):
            rows = slice(n * blk, (n + 1) * blk)
            raw = lax.dot_general(k_ref[rows, :], q, tb, preferred_element_type=F32)
            if n == qi:
                raw = jnp.where(causal, raw, NEG_BIG)
            s_scr[slot, rows, :] = raw
            cm = jnp.max(raw, axis=0, keepdims=True) * scale_l2
            if bias is not None and n < qi:
                cm = cm + bias[n:n + 1, :]
            m = cm if m is None else jnp.maximum(m, cm)

        l = None
        for n in range(qi + 1):
            rows = slice(n * blk, (n + 1) * blk)
            shift = -m
            if bias is not None and n < qi:
                shift = bias[n:n + 1, :] - m
            p = jnp.exp2(s_scr[slot, rows, :] * scale_l2 + shift)
            ps = jnp.sum(p, axis=0, keepdims=True)
            l = ps if l is None else l + ps
            p_scr[slot, rows, :] = p.astype(BF16)

        acc = jnp.dot(vt_scr[:, 0:nk], p_scr[slot, 0:nk, :], preferred_element_type=F32)
        o_ref[qi * blk:(qi + 1) * blk, :] = (acc / l).T


def _moba(qkv, batch, seq):
    t = qkv.shape[0]
    nb = seq // MOBA_BLOCK
    nh = N_ATTN_HEADS
    return pl.pallas_call(
        functools.partial(_moba_kernel, nb=nb),
        grid=(batch, nh),
        in_specs=[
            pl.BlockSpec((seq, HEAD_DIM), lambda b, h: (b, h)),
            pl.BlockSpec((seq, HEAD_DIM), lambda b, h: (b, nh + h)),
            pl.BlockSpec((seq, HEAD_DIM), lambda b, h: (b, 2 * nh + h)),
        ],
        out_specs=pl.BlockSpec((seq, HEAD_DIM), lambda b, h: (b, h)),
        out_shape=jax.ShapeDtypeStruct((t, nh * HEAD_DIM), F32),
        scratch_shapes=[
            pltpu.VMEM((HEAD_DIM, seq), BF16),
            pltpu.VMEM((nb, HEAD_DIM), F32),
            pltpu.VMEM((2, seq, MOBA_BLOCK), F32),
            pltpu.VMEM((2, seq, MOBA_BLOCK), BF16),
        ],
        compiler_params=pltpu.CompilerParams(
            dimension_semantics=("parallel", "parallel"), vmem_limit_bytes=VMEM_LIMIT),
        name="moba",
    )(qkv, qkv, qkv)


def _rglru_kernel(xr_ref, gr_ref, cw_ref, cb_ref, wg_ref, ba_ref, bx_ref, lam_ref, o_ref,
                  xbuf, hcar, a_scr, b_scr, *, tl, cw):
    t = pl.program_id(2)
    pad = SUBLANES

    @pl.when(t == 0)
    def _():
        xbuf[0:pad, :] = jnp.zeros((pad, cw), F32)
        hcar[...] = jnp.zeros_like(hcar)

    xbuf[pad:pad + tl, :] = xr_ref[...]
    w = cw_ref[...]
    xc = cb_ref[...] + w[CONV_WIDTH - 1:CONV_WIDTH] * xbuf[pad:pad + tl, :]
    for j in range(1, CONV_WIDTH):
        xc = xc + w[CONV_WIDTH - 1 - j:CONV_WIDTH - j] * xbuf[pad - j:pad - j + tl, :]
    xbuf[0:pad, :] = xbuf[tl:tl + pad, :]

    gates = jnp.dot(xc.astype(BF16), wg_ref[0], preferred_element_type=F32)
    r = jax.nn.sigmoid(gates[:, :cw] + ba_ref[...])
    i = jax.nn.sigmoid(gates[:, cw:] + bx_ref[...])
    z = -lam_ref[...]
    softplus = jnp.maximum(z, 0.0) + jnp.log1p(jnp.exp(-jnp.abs(z)))
    log_a = (-RGLRU_C * r) * softplus
    a = jnp.exp(log_a)
    b = jnp.sqrt(-jnp.tanh(log_a) * (a * a + 1.0)) * (i * xc)

    row8 = lax.broadcasted_iota(jnp.int32, (tl, cw), 0) & (SUBLANES - 1)
    for s in (1, 2, 4):
        ar = pltpu.roll(a, s, 0)
        br = pltpu.roll(b, s, 0)
        keep = row8 >= s
        b = jnp.where(keep, a * br + b, b)
        a = jnp.where(keep, a * ar, a)
    a_scr[...] = a
    b_scr[...] = b

    def group(gidx, carry):
        r0 = pl.multiple_of(gidx * SUBLANES, SUBLANES)
        hg = a_scr[pl.ds(r0, SUBLANES), :] * carry + b_scr[pl.ds(r0, SUBLANES), :]
        b_scr[pl.ds(r0, SUBLANES), :] = hg
        return hg[SUBLANES - 1:SUBLANES, :]

    hcar[...] = lax.fori_loop(0, tl // SUBLANES, group, hcar[...], unroll=8)
    o_ref[...] = b_scr[...] * jax.nn.gelu(gr_ref[...])


def _block_diag_gates(w_a, w_x, cw):
    nblk, bd, _ = w_a.shape
    per = cw // bd
    nt = nblk // per
    eye = jnp.eye(per, dtype=w_a.dtype)

    def bdiag(w):
        w = w.reshape(nt, per, bd, bd)
        full = jnp.einsum("tpij,pq->tpiqj", w, eye)
        return full.reshape(nt, cw, cw)

    return jnp.concatenate([bdiag(w_a), bdiag(w_x)], axis=2).astype(BF16)


def _rglru(rec_in, conv_w, conv_b, w_a, b_a, w_x, b_x, lam, batch, seq, *, tl=512, cw=256):
    t = rec_in.shape[0]
    r_width = conv_w.shape[1]
    nc = r_width // cw
    nt = seq // tl
    wg = _block_diag_gates(w_a, w_x, cw)
    row = lambda v: v.reshape(1, r_width)
    vec_spec = pl.BlockSpec((1, cw), lambda b, c, s: (0, c))
    return pl.pallas_call(
        functools.partial(_rglru_kernel, tl=tl, cw=cw),
        grid=(batch, nc, nt),
        in_specs=[
            pl.BlockSpec((tl, cw), lambda b, c, s: (b * nt + s, c)),
            pl.BlockSpec((tl, cw), lambda b, c, s: (b * nt + s, nc + c)),
            pl.BlockSpec((CONV_WIDTH, cw), lambda b, c, s: (0, c)),
            vec_spec,
            pl.BlockSpec((1, cw, 2 * cw), lambda b, c, s: (c, 0, 0)),
            vec_spec, vec_spec, vec_spec,
        ],
        out_specs=pl.BlockSpec((tl, cw), lambda b, c, s: (b * nt + s, c)),
        out_shape=jax.ShapeDtypeStruct((t, r_width), F32),
        scratch_shapes=[
            pltpu.VMEM((tl + SUBLANES, cw), F32),
            pltpu.VMEM((1, cw), F32),
            pltpu.VMEM((tl, cw), F32),
            pltpu.VMEM((tl, cw), F32),
        ],
        compiler_params=pltpu.CompilerParams(
            dimension_semantics=("parallel", "parallel", "arbitrary"), vmem_limit_bytes=VMEM_LIMIT),
        name="rglru",
    )(rec_in, rec_in, conv_w, row(conv_b), wg, row(b_a), row(b_x), row(lam))


def _out_proj_kernel(x_ref, attn_ref, rec_ref, ga_ref, gr_ref, w_ref, pg_ref, o_ref):
    a = _rms(attn_ref[...], ga_ref[...]).astype(BF16)
    r = _rms(rec_ref[...], gr_ref[...]).astype(BF16)
    aw = a.shape[1]
    y = (jnp.dot(a, w_ref[:aw, :], preferred_element_type=F32)
         + jnp.dot(r, w_ref[aw:, :], preferred_element_type=F32))
    o_ref[...] = x_ref[...] + _rms(y, pg_ref[...])


def _out_proj(x2d, attn, rec, ga, gr, w_out, pg, *, tm=512):
    t, d = x2d.shape
    aw = attn.shape[1]
    rw = rec.shape[1]
    return pl.pallas_call(
        _out_proj_kernel,
        grid=(t // tm,),
        in_specs=[
            pl.BlockSpec((tm, d), lambda i: (i, 0)),
            pl.BlockSpec((tm, aw), lambda i: (i, 0)),
            pl.BlockSpec((tm, rw), lambda i: (i, 0)),
            pl.BlockSpec((1, aw), lambda i: (0, 0)),
            pl.BlockSpec((1, rw), lambda i: (0, 0)),
            pl.BlockSpec((aw + rw, d), lambda i: (0, 0)),
            pl.BlockSpec((1, d), lambda i: (0, 0)),
        ],
        out_specs=pl.BlockSpec((tm, d), lambda i: (i, 0)),
        out_shape=jax.ShapeDtypeStruct((t, d), F32),
        compiler_params=pltpu.CompilerParams(
            dimension_semantics=("parallel",), vmem_limit_bytes=VMEM_LIMIT),
        name="out_proj",
    )(x2d, attn, rec, ga.reshape(1, aw), gr.reshape(1, rw), w_out.astype(BF16), pg.reshape(1, d))


def kernel(x, ffn1_pre_g, ffn1_w_gate_up, ffn1_w_down, ffn1_post_g, mix_pre_g, w_in, conv_w, conv_b, rg_w_a, rg_b_a, rg_w_x, rg_b_x, rg_lambda, attn_out_g, rec_out_g, w_out, mix_post_g, ffn2_pre_g, ffn2_w_gate_up, ffn2_w_down, ffn2_post_g):
    batch, seq, d = x.shape
    depth = ffn1_pre_g.shape[0]
    attn_width = N_ATTN_HEADS * HEAD_DIM
    rec_width = conv_w.shape[-1]
    assert w_in.shape[-1] == 3 * attn_width + 2 * rec_width
    assert seq % 512 == 0 and seq % MOBA_BLOCK == 0
    h = x.reshape(batch * seq, d)
    for l in range(depth):
        h = _ffn(h, ffn1_pre_g[l], ffn1_w_gate_up[l], ffn1_w_down[l], ffn1_post_g[l])
        qkv, rec_in = _in_proj(h, mix_pre_g[l], w_in[l], seq, attn_width, rec_width)
        attn = _moba(qkv, batch, seq)
        rec = _rglru(rec_in, conv_w[l], conv_b[l], rg_w_a[l], rg_b_a[l], rg_w_x[l], rg_b_x[l],
                     rg_lambda[l], batch, seq)
        h = _out_proj(h, attn, rec, attn_out_g[l], rec_out_g[l], w_out[l], mix_post_g[l])
        h = _ffn(h, ffn2_pre_g[l], ffn2_w_gate_up[l], ffn2_w_down[l], ffn2_post_g[l])
    return h.reshape(batch, seq, d)
```

```python
import functools
import math

import jax
import jax.numpy as jnp
from jax import lax
from jax.experimental import pallas as pl
from jax.experimental.pallas import tpu as pltpu

F32 = jnp.float32
BF16 = jnp.bfloat16

N_ATTN_HEADS = 8
HEAD_DIM = 128
ROT_DIM = 32
ROPE_THETA = 500000.0
MOBA_BLOCK = 256
MOBA_TOPK = 3
N_REC_BLOCKS = 16
CONV_WIDTH = 4
RGLRU_C = 8.0
FFN_HALF = 0.5
EPS = 1e-6

V7X_VMEM_BYTES = 64 * 1024 * 1024
VMEM_LIMIT = 56 * 1024 * 1024
LANES = 128
SUBLANES = 8

NEG_BIG = -1e30
LOG2E = 1.4426950408889634


def _rms(x, g):
    ms = jnp.mean(x * x, axis=-1, keepdims=True)
    return x * lax.rsqrt(ms + EPS) * g


def _ffn_kernel(x_ref, preg_ref, wg0_ref, wg1_ref, wu0_ref, wu1_ref, wd0_ref, wd1_ref, postg_ref,
                o_ref, h_scr, *, n_last):
    j = pl.program_id(1)
    last = pl.num_programs(1) - 1

    @pl.when(j == 0)
    def _():
        h_scr[...] = _rms(x_ref[...], preg_ref[...]).astype(BF16)
        o_ref[...] = jnp.zeros_like(o_ref)

    def chunk(n_blocks):
        h = h_scr[...]
        wg = [wg0_ref, wg1_ref][:n_blocks]
        wu = [wu0_ref, wu1_ref][:n_blocks]
        wd = [wd0_ref, wd1_ref][:n_blocks]
        cat = lambda refs, axis: jnp.concatenate([r[...].astype(BF16) for r in refs], axis=axis)
        gu = jnp.dot(h, cat(wg + wu, 1), preferred_element_type=F32)
        g = gu[:, :n_blocks * LANES]
        u = gu[:, n_blocks * LANES:]
        a = (g * jax.nn.sigmoid(g) * u).astype(BF16)
        o_ref[...] += jnp.dot(a, cat(wd, 0), preferred_element_type=F32)

    @pl.when(j < last)
    def _():
        chunk(2)

    @pl.when(j == last)
    def _():
        chunk(n_last)
        o_ref[...] = x_ref[...] + FFN_HALF * _rms(o_ref[...], postg_ref[...])


def _ffn(x2d, pre_g, w_gate_up, w_down, post_g, *, tm=1024):
    t, d = x2d.shape
    f = w_down.shape[0]
    assert f % LANES == 0
    nblk = f // LANES
    n_steps = -(-nblk // 2)
    n_last = nblk - 2 * (n_steps - 1)
    col = lambda k, base: pl.BlockSpec(
        (d, LANES), lambda i, j: (0, base + jnp.minimum(2 * j + k, nblk - 1)))
    row = lambda k: pl.BlockSpec((LANES, d), lambda i, j: (jnp.minimum(2 * j + k, nblk - 1), 0))
    return pl.pallas_call(
        functools.partial(_ffn_kernel, n_last=n_last),
        grid=(t // tm, n_steps),
        in_specs=[
            pl.BlockSpec((tm, d), lambda i, j: (i, 0)),
            pl.BlockSpec((1, d), lambda i, j: (0, 0)),
            col(0, 0), col(1, 0), col(0, nblk), col(1, nblk),
            row(0), row(1),
            pl.BlockSpec((1, d), lambda i, j: (0, 0)),
        ],
        out_specs=pl.BlockSpec((tm, d), lambda i, j: (i, 0)),
        out_shape=jax.ShapeDtypeStruct((t, d), F32),
        scratch_shapes=[pltpu.VMEM((tm, d), BF16)],
        compiler_params=pltpu.CompilerParams(
            dimension_semantics=("parallel", "arbitrary"), vmem_limit_bytes=VMEM_LIMIT),
        name="ffn",
    )(x2d, pre_g.reshape(1, d), w_gate_up, w_gate_up, w_gate_up, w_gate_up, w_down, w_down,
      post_g.reshape(1, d))


def _in_proj_kernel(x_ref, g_ref, w_ref, cos_ref, sin_ref, qkv_ref, rec_ref,
                    *, tn, n_rope_tiles, n_qkv_tiles, n_tiles):
    h = _rms(x_ref[...], g_ref[...]).astype(BF16)
    c = cos_ref[...]
    s = sin_ref[...]
    lane = lax.broadcasted_iota(jnp.int32, c.shape, 1)
    first_half = lane < ROT_DIM // 2
    for j in range(n_tiles):
        y = jnp.dot(h, w_ref[:, j * tn:(j + 1) * tn], preferred_element_type=F32)
        if j < n_rope_tiles:
            for hh in range(tn // HEAD_DIM):
                yh = y[:, hh * HEAD_DIM:(hh + 1) * HEAD_DIM]
                partner = jnp.where(first_half,
                                    pltpu.roll(yh, HEAD_DIM - ROT_DIM // 2, 1),
                                    pltpu.roll(yh, ROT_DIM // 2, 1))
                col = j * tn + hh * HEAD_DIM
                qkv_ref[:, col:col + HEAD_DIM] = (yh * c + partner * s).astype(BF16)
        elif j < n_qkv_tiles:
            qkv_ref[:, j * tn:(j + 1) * tn] = y.astype(BF16)
        else:
            rec_ref[:, (j - n_qkv_tiles) * tn:(j - n_qkv_tiles + 1) * tn] = y


def _rope_tables(seq):
    inv_freq = ROPE_THETA ** (-jnp.arange(0, ROT_DIM, 2, dtype=F32) / ROT_DIM)
    ang = jnp.arange(seq, dtype=jnp.int32).astype(F32)[:, None] * inv_freq[None, :]
    cos, sin = jnp.cos(ang), jnp.sin(ang)
    ones = jnp.ones((seq, HEAD_DIM - ROT_DIM), F32)
    zeros = jnp.zeros((seq, HEAD_DIM - ROT_DIM), F32)
    cos_t = jnp.concatenate([cos, cos, ones], axis=1)
    sin_t = jnp.concatenate([-sin, sin, zeros], axis=1)
    return cos_t, sin_t


def _in_proj(x2d, g, w_in, seq, attn_width, rec_width, *, tm=256, tn=512):
    t, d = x2d.shape
    cols = w_in.shape[1]
    n_tiles = cols // tn
    n_rope_tiles = 2 * attn_width // tn
    n_qkv_tiles = 3 * attn_width // tn
    s_tiles = seq // tm
    cos_t, sin_t = _rope_tables(seq)
    kern = functools.partial(_in_proj_kernel, tn=tn, n_rope_tiles=n_rope_tiles,
                             n_qkv_tiles=n_qkv_tiles, n_tiles=n_tiles)
    return pl.pallas_call(
        kern,
        grid=(t // tm,),
        in_specs=[
            pl.BlockSpec((tm, d), lambda i: (i, 0)),
            pl.BlockSpec((1, d), lambda i: (0, 0)),
            pl.BlockSpec((d, cols), lambda i: (0, 0), pipeline_mode=pl.Buffered(1)),
            pl.BlockSpec((tm, HEAD_DIM), lambda i: (i % s_tiles, 0)),
            pl.BlockSpec((tm, HEAD_DIM), lambda i: (i % s_tiles, 0)),
        ],
        out_specs=[
            pl.BlockSpec((tm, 3 * attn_width), lambda i: (i, 0)),
            pl.BlockSpec((tm, 2 * rec_width), lambda i: (i, 0)),
        ],
        out_shape=[
            jax.ShapeDtypeStruct((t, 3 * attn_width), BF16),
            jax.ShapeDtypeStruct((t, 2 * rec_width), F32),
        ],
        compiler_params=pltpu.CompilerParams(
            dimension_semantics=("parallel",), vmem_limit_bytes=VMEM_LIMIT),
        name="in_proj",
    )(x2d, g.reshape(1, d), w_in.astype(BF16), cos_t, sin_t)


def _moba_kernel(q_ref, k_ref, v_ref, o_ref, vt_scr, kmean_scr, s_scr, p_scr, *, nb):
    blk = MOBA_BLOCK
    scale_l2 = (HEAD_DIM ** -0.5) * LOG2E
    tb = (((1,), (1,)), ((), ()))

    ones_row = lax.broadcasted_iota(jnp.int32, (vt_scr.shape[0] - HEAD_DIM, vt_scr.shape[1]), 0) == 0
    vt_scr[HEAD_DIM:, :] = jnp.where(ones_row, 1.0, 0.0).astype(BF16)
    for n in range(nb):
        rows = slice(n * blk, (n + 1) * blk)
        vt_scr[:HEAD_DIM, rows] = v_ref[rows, :].astype(F32).T.astype(BF16)
        kmean_scr[n:n + 1, :] = jnp.mean(k_ref[rows, :].astype(F32), axis=0, keepdims=True)

    km = kmean_scr[...]
    km_hi = km.astype(BF16)
    km_lo = (km - km_hi.astype(F32)).astype(BF16)
    blk_id = lax.broadcasted_iota(jnp.int32, (nb, blk), 0)
    key_pos = lax.broadcasted_iota(jnp.int32, (blk, blk), 0)
    qry_pos = lax.broadcasted_iota(jnp.int32, (blk, blk), 1)
    causal = key_pos <= qry_pos

    for qi in range(nb):
        slot = qi % 2
        nk = (qi + 1) * blk
        q = q_ref[qi * blk:(qi + 1) * blk, :]

        if qi <= MOBA_TOPK:
            bias = None
        else:
            gate = (lax.dot_general(km_hi, q, tb, preferred_element_type=F32)
                    + lax.dot_general(km_lo, q, tb, preferred_element_type=F32))
            eligible = blk_id < qi
            rem = jnp.where(eligible, gate, -jnp.inf)
            sel = jnp.zeros((nb, blk), jnp.bool_)
            for _t in range(MOBA_TOPK):
                mx = jnp.max(rem, axis=0, keepdims=True)
                first = jnp.min(jnp.where(rem == mx, blk_id, nb), axis=0, keepdims=True)
                pick = blk_id == first
                sel = jnp.logical_or(sel, pick)
                rem = jnp.where(pick, -jnp.inf, rem)
            bias = jnp.where(jnp.logical_and(sel, eligible), 0.0, NEG_BIG)

        m = None
        for n in range(qi + 1):
            rows = slice(n * blk, (n + 1) * blk)
            raw = lax.dot_general(k_ref[rows, :], q, tb, preferred_element_type=F32)
            if n == qi:
                raw = jnp.where(causal, raw, NEG_BIG)
            s_scr[slot, rows, :] = raw
            cm = jnp.max(raw, axis=0, keepdims=True) * scale_l2
            if bias is not None and n < qi:
                cm = cm + bias[n:n + 1, :]
            m = cm if m is None else jnp.maximum(m, cm)

        for n in range(qi + 1):
            rows = slice(n * blk, (n + 1) * blk)
            shift = -m
            if bias is not None and n < qi:
                shift = bias[n:n + 1, :] - m
            x = s_scr[slot, rows, :] * scale_l2 + shift
            p_scr[slot, rows, :] = jnp.exp2(x.astype(BF16))

        acc = jnp.dot(vt_scr[:, 0:nk], p_scr[slot, 0:nk, :], preferred_element_type=F32)
        l = acc[HEAD_DIM:HEAD_DIM + 1, :]
        o_ref[qi * blk:(qi + 1) * blk, :] = (acc[:HEAD_DIM, :] / l).T


def _moba(qkv, batch, seq):
    t = qkv.shape[0]
    nb = seq // MOBA_BLOCK
    nh = N_ATTN_HEADS
    return pl.pallas_call(
        functools.partial(_moba_kernel, nb=nb),
        grid=(batch, nh),
        in_specs=[
            pl.BlockSpec((seq, HEAD_DIM), lambda b, h: (b, h)),
            pl.BlockSpec((seq, HEAD_DIM), lambda b, h: (b, nh + h)),
            pl.BlockSpec((seq, HEAD_DIM), lambda b, h: (b, 2 * nh + h)),
        ],
        out_specs=pl.BlockSpec((seq, HEAD_DIM), lambda b, h: (b, h)),
        out_shape=jax.ShapeDtypeStruct((t, nh * HEAD_DIM), F32),
        scratch_shapes=[
            pltpu.VMEM((HEAD_DIM + 2 * SUBLANES, seq), BF16),
            pltpu.VMEM((nb, HEAD_DIM), F32),
            pltpu.VMEM((2, seq, MOBA_BLOCK), F32),
            pltpu.VMEM((2, seq, MOBA_BLOCK), BF16),
        ],
        compiler_params=pltpu.CompilerParams(
            dimension_semantics=("parallel", "parallel"), vmem_limit_bytes=VMEM_LIMIT),
        name="moba",
    )(qkv, qkv, qkv)


def _rglru_kernel(xr_ref, gr_ref, cw_ref, cb_ref, wg_ref, ba_ref, bx_ref, lam_ref, o_ref,
                  xbuf, hcar, a_scr, b_scr, *, tl, cw):
    t = pl.program_id(2)
    pad = SUBLANES

    @pl.when(t == 0)
    def _():
        xbuf[0:pad, :] = jnp.zeros((pad, cw), F32)
        hcar[...] = jnp.zeros_like(hcar)

    xbuf[pad:pad + tl, :] = xr_ref[...]
    w = cw_ref[...]
    xc = cb_ref[...] + w[CONV_WIDTH - 1:CONV_WIDTH] * xbuf[pad:pad + tl, :]
    for j in range(1, CONV_WIDTH):
        xc = xc + w[CONV_WIDTH - 1 - j:CONV_WIDTH - j] * xbuf[pad - j:pad - j + tl, :]
    xbuf[0:pad, :] = xbuf[tl:tl + pad, :]

    gates = jnp.dot(xc.astype(BF16), wg_ref[0], preferred_element_type=F32)
    r = jax.nn.sigmoid(gates[:, :cw] + ba_ref[...])
    i = jax.nn.sigmoid(gates[:, cw:] + bx_ref[...])
    z = -lam_ref[...]
    softplus = jnp.maximum(z, 0.0) + jnp.log1p(jnp.exp(-jnp.abs(z)))
    log_a = (-RGLRU_C * r) * softplus
    a = jnp.exp(log_a)
    b = jnp.sqrt(-jnp.tanh(log_a) * (a * a + 1.0)) * (i * xc)

    row8 = lax.broadcasted_iota(jnp.int32, (tl, cw), 0) & (SUBLANES - 1)
    for s in (1, 2, 4):
        ar = pltpu.roll(a, s, 0)
        br = pltpu.roll(b, s, 0)
        keep = row8 >= s
        b = jnp.where(keep, a * br + b, b)
        a = jnp.where(keep, a * ar, a)
    a_scr[...] = a
    b_scr[...] = b

    def group(gidx, carry):
        r0 = pl.multiple_of(gidx * SUBLANES, SUBLANES)
        hg = a_scr[pl.ds(r0, SUBLANES), :] * carry + b_scr[pl.ds(r0, SUBLANES), :]
        b_scr[pl.ds(r0, SUBLANES), :] = hg
        return hg[SUBLANES - 1:SUBLANES, :]

    hcar[...] = lax.fori_loop(0, tl // SUBLANES, group, hcar[...], unroll=8)
    o_ref[...] = b_scr[...] * jax.nn.gelu(gr_ref[...])


def _block_diag_gates(w_a, w_x, cw):
    nblk, bd, _ = w_a.shape
    per = cw // bd
    nt = nblk // per
    eye = jnp.eye(per, dtype=w_a.dtype)

    def bdiag(w):
        w = w.reshape(nt, per, bd, bd)
        full = jnp.einsum("tpij,pq->tpiqj", w, eye)
        return full.reshape(nt, cw, cw)

    return jnp.concatenate([bdiag(w_a), bdiag(w_x)], axis=2).astype(BF16)


def _rglru(rec_in, conv_w, conv_b, w_a, b_a, w_x, b_x, lam, batch, seq, *, tl=512, cw=256):
    t = rec_in.shape[0]
    r_width = conv_w.shape[1]
    nc = r_width // cw
    nt = seq // tl
    wg = _block_diag_gates(w_a, w_x, cw)
    row = lambda v: v.reshape(1, r_width)
    vec_spec = pl.BlockSpec((1, cw), lambda b, c, s: (0, c))
    return pl.pallas_call(
        functools.partial(_rglru_kernel, tl=tl, cw=cw),
        grid=(batch, nc, nt),
        in_specs=[
            pl.BlockSpec((tl, cw), lambda b, c, s: (b * nt + s, c)),
            pl.BlockSpec((tl, cw), lambda b, c, s: (b * nt + s, nc + c)),
            pl.BlockSpec((CONV_WIDTH, cw), lambda b, c, s: (0, c)),
            vec_spec,
            pl.BlockSpec((1, cw, 2 * cw), lambda b, c, s: (c, 0, 0)),
            vec_spec, vec_spec, vec_spec,
        ],
        out_specs=pl.BlockSpec((tl, cw), lambda b, c, s: (b * nt + s, c)),
        out_shape=jax.ShapeDtypeStruct((t, r_width), F32),
        scratch_shapes=[
            pltpu.VMEM((tl + SUBLANES, cw), F32),
            pltpu.VMEM((1, cw), F32),
            pltpu.VMEM((tl, cw), F32),
            pltpu.VMEM((tl, cw), F32),
        ],
        compiler_params=pltpu.CompilerParams(
            dimension_semantics=("parallel", "parallel", "arbitrary"), vmem_limit_bytes=VMEM_LIMIT),
        name="rglru",
    )(rec_in, rec_in, conv_w, row(conv_b), wg, row(b_a), row(b_x), row(lam))


def _out_proj_kernel(x_ref, attn_ref, rec_ref, ga_ref, gr_ref, w_ref, pg_ref, o_ref):
    a = _rms(attn_ref[...], ga_ref[...]).astype(BF16)
    r = _rms(rec_ref[...], gr_ref[...]).astype(BF16)
    aw = a.shape[1]
    y = (jnp.dot(a, w_ref[:aw, :], preferred_element_type=F32)
         + jnp.dot(r, w_ref[aw:, :], preferred_element_type=F32))
    o_ref[...] = x_ref[...] + _rms(y, pg_ref[...])


def _out_proj(x2d, attn, rec, ga, gr, w_out, pg, *, tm=512):
    t, d = x2d.shape
    aw = attn.shape[1]
    rw = rec.shape[1]
    return pl.pallas_call(
        _out_proj_kernel,
        grid=(t // tm,),
        in_specs=[
            pl.BlockSpec((tm, d), lambda i: (i, 0)),
            pl.BlockSpec((tm, aw), lambda i: (i, 0)),
            pl.BlockSpec((tm, rw), lambda i: (i, 0)),
            pl.BlockSpec((1, aw), lambda i: (0, 0)),
            pl.BlockSpec((1, rw), lambda i: (0, 0)),
            pl.BlockSpec((aw + rw, d), lambda i: (0, 0)),
            pl.BlockSpec((1, d), lambda i: (0, 0)),
        ],
        out_specs=pl.BlockSpec((tm, d), lambda i: (i, 0)),
        out_shape=jax.ShapeDtypeStruct((t, d), F32),
        compiler_params=pltpu.CompilerParams(
            dimension_semantics=("parallel",), vmem_limit_bytes=VMEM_LIMIT),
        name="out_proj",
    )(x2d, attn, rec, ga.reshape(1, aw), gr.reshape(1, rw), w_out.astype(BF16), pg.reshape(1, d))


def kernel(x, ffn1_pre_g, ffn1_w_gate_up, ffn1_w_down, ffn1_post_g, mix_pre_g, w_in, conv_w, conv_b, rg_w_a, rg_b_a, rg_w_x, rg_b_x, rg_lambda, attn_out_g, rec_out_g, w_out, mix_post_g, ffn2_pre_g, ffn2_w_gate_up, ffn2_w_down, ffn2_post_g):
    batch, seq, d = x.shape
    depth = ffn1_pre_g.shape[0]
    attn_width = N_ATTN_HEADS * HEAD_DIM
    rec_width = conv_w.shape[-1]
    assert w_in.shape[-1] == 3 * attn_width + 2 * rec_width
    assert seq % 512 == 0 and seq % MOBA_BLOCK == 0
    h = x.reshape(batch * seq, d)
    for l in range(depth):
        h = _ffn(h, ffn1_pre_g[l], ffn1_w_gate_up[l], ffn1_w_down[l], ffn1_post_g[l])
        qkv, rec_in = _in_proj(h, mix_pre_g[l], w_in[l], seq, attn_width, rec_width)
        attn = _moba(qkv, batch, seq)
        rec = _rglru(rec_in, conv_w[l], conv_b[l], rg_w_a[l], rg_b_a[l], rg_w_x[l], rg_b_x[l],
                     rg_lambda[l], batch, seq)
        h = _out_proj(h, attn, rec, attn_out_g[l], rec_out_g[l], w_out[l], mix_post_g[l])
        h = _ffn(h, ffn2_pre_g[l], ffn2_w_gate_up[l], ffn2_w_down[l], ffn2_post_g[l])
    return h.reshape(batch, seq, d)
```

```python
import functools

import jax
import jax.numpy as jnp
from jax import lax
from jax.experimental import pallas as pl
from jax.experimental.pallas import tpu as pltpu

F32 = jnp.float32
BF16 = jnp.bfloat16

N_ATTN_HEADS = 8
HEAD_DIM = 128
ROT_DIM = 32
ROPE_THETA = 500000.0
MOBA_BLOCK = 256
MOBA_TOPK = 3
CONV_WIDTH = 4
RGLRU_C = 8.0
FFN_HALF = 0.5
EPS = 1e-6

VMEM_LIMIT = 56 * 1024 * 1024
LANES = 128
SUBLANES = 8

NEG_BIG = -1e30
LOG2E = 1.4426950408889634


def _rms(x, g):
    ms = jnp.mean(x * x, axis=-1, keepdims=True)
    return x * lax.rsqrt(ms + EPS) * g


def _ffn_kernel(x_ref, preg_ref, wg0_ref, wg1_ref, wu0_ref, wu1_ref, wd0_ref, wd1_ref, postg_ref,
                o_ref, h_scr, *, n_last, row_blk):
    j = pl.program_id(1)
    last = pl.num_programs(1) - 1
    row_blocks = [slice(r, r + row_blk) for r in range(0, x_ref.shape[0], row_blk)]

    def weights(n_blocks):
        cat = lambda refs, axis: jnp.concatenate([r[...].astype(BF16) for r in refs], axis=axis)
        wgu = cat([wg0_ref, wg1_ref][:n_blocks] + [wu0_ref, wu1_ref][:n_blocks], 1)
        return wgu, cat([wd0_ref, wd1_ref][:n_blocks], 0)

    def swiglu_down(h, wgu, wdn):
        gu = jnp.dot(h, wgu, preferred_element_type=F32)
        half = gu.shape[1] // 2
        g = gu[:, :half]
        u = gu[:, half:]
        a = (g * jax.nn.sigmoid(g) * u).astype(BF16)
        return jnp.dot(a, wdn, preferred_element_type=F32)

    @pl.when(j == 0)
    def _():
        wgu, wdn = weights(2)
        for rows in row_blocks:
            h = _rms(x_ref[rows, :], preg_ref[...]).astype(BF16)
            h_scr[rows, :] = h
            o_ref[rows, :] = swiglu_down(h, wgu, wdn)

    @pl.when(jnp.logical_and(j > 0, j < last))
    def _():
        wgu, wdn = weights(2)
        o_ref[...] += swiglu_down(h_scr[...], wgu, wdn)

    @pl.when(j == last)
    def _():
        wgu, wdn = weights(n_last)
        for rows in row_blocks:
            acc = o_ref[rows, :] + swiglu_down(h_scr[rows, :], wgu, wdn)
            o_ref[rows, :] = x_ref[rows, :] + FFN_HALF * _rms(acc, postg_ref[...])


def _ffn(x2d, pre_g, w_gate_up, w_down, post_g, *, tm=1024, row_blk=256):
    t, d = x2d.shape
    f = w_down.shape[0]
    assert f % LANES == 0
    nblk = f // LANES
    n_steps = -(-nblk // 2)
    n_last = nblk - 2 * (n_steps - 1)
    assert n_steps >= 2
    col = lambda k, base: pl.BlockSpec(
        (d, LANES), lambda i, j: (0, base + jnp.minimum(2 * j + k, nblk - 1)))
    row = lambda k: pl.BlockSpec((LANES, d), lambda i, j: (jnp.minimum(2 * j + k, nblk - 1), 0))
    return pl.pallas_call(
        functools.partial(_ffn_kernel, n_last=n_last, row_blk=row_blk),
        grid=(t // tm, n_steps),
        in_specs=[
            pl.BlockSpec((tm, d), lambda i, j: (i, 0)),
            pl.BlockSpec((1, d), lambda i, j: (0, 0)),
            col(0, 0), col(1, 0), col(0, nblk), col(1, nblk),
            row(0), row(1),
            pl.BlockSpec((1, d), lambda i, j: (0, 0)),
        ],
        out_specs=pl.BlockSpec((tm, d), lambda i, j: (i, 0)),
        out_shape=jax.ShapeDtypeStruct((t, d), F32),
        scratch_shapes=[pltpu.VMEM((tm, d), BF16)],
        compiler_params=pltpu.CompilerParams(
            dimension_semantics=("parallel", "arbitrary"), vmem_limit_bytes=VMEM_LIMIT),
        name="ffn",
    )(x2d, pre_g.reshape(1, d), w_gate_up, w_gate_up, w_gate_up, w_gate_up, w_down, w_down,
      post_g.reshape(1, d))


def _in_proj_kernel(x_ref, g_ref, w_ref, cos_ref, sin_ref, cw_ref, cb_ref, wg_ref, ba_ref, bx_ref,
                    lam_ref, qkv_ref, a_ref, b_ref, gg_ref, xbuf,
                    *, tn, attn_width, rec_width, gate_tile, s_tiles):
    i = pl.program_id(0)
    tm = x_ref.shape[0]
    pad = SUBLANES

    @pl.when(i % s_tiles == 0)
    def _():
        xbuf[0:pad, :] = jnp.zeros((pad, rec_width), F32)

    h = _rms(x_ref[...], g_ref[...]).astype(BF16)
    c = cos_ref[...]
    s = sin_ref[...]
    lane = lax.broadcasted_iota(jnp.int32, c.shape, 1)
    first_half = lane < ROT_DIM // 2

    def proj(col0):
        return jnp.dot(h, w_ref[:, col0:col0 + tn], preferred_element_type=F32)

    def emit_qk(col0):
        y = proj(col0)
        for hh in range(tn // HEAD_DIM):
            yh = y[:, hh * HEAD_DIM:(hh + 1) * HEAD_DIM]
            partner = jnp.where(first_half,
                                pltpu.roll(yh, HEAD_DIM - ROT_DIM // 2, 1),
                                pltpu.roll(yh, ROT_DIM // 2, 1))
            col = col0 + hh * HEAD_DIM
            qkv_ref[:, col:col + HEAD_DIM] = (yh * c + partner * s).astype(BF16)

    def emit_v(col0):
        qkv_ref[:, col0:col0 + tn] = proj(col0).astype(BF16)

    def emit_gate_gelu(col0):
        gg_ref[:, col0:col0 + tn] = jax.nn.gelu(proj(3 * attn_width + rec_width + col0))

    w = cw_ref[...]

    def emit_recurrence_inputs(ct):
        cols = slice(ct * gate_tile, (ct + 1) * gate_tile)
        xc = cb_ref[:, cols] + w[CONV_WIDTH - 1:CONV_WIDTH, cols] * xbuf[pad:pad + tm, cols]
        for j in range(1, CONV_WIDTH):
            xc = xc + w[CONV_WIDTH - 1 - j:CONV_WIDTH - j, cols] * xbuf[pad - j:pad - j + tm, cols]
        gates = jnp.dot(xc.astype(BF16), wg_ref[ct], preferred_element_type=F32)
        r = jax.nn.sigmoid(gates[:, :gate_tile] + ba_ref[:, cols])
        gi = jax.nn.sigmoid(gates[:, gate_tile:] + bx_ref[:, cols])
        z = -lam_ref[:, cols]
        softplus = jnp.maximum(z, 0.0) + jnp.log1p(jnp.exp(-jnp.abs(z)))
        log_a = (-RGLRU_C * r) * softplus
        a = jnp.exp(log_a)
        a_ref[:, cols] = a
        b_ref[:, cols] = jnp.sqrt(-jnp.tanh(log_a) * (a * a + 1.0)) * (gi * xc)

    for col0 in range(0, rec_width, tn):
        xbuf[pad:pad + tm, col0:col0 + tn] = proj(3 * attn_width + col0)
    matmul_work = ([functools.partial(emit_qk, c0) for c0 in range(0, 2 * attn_width, tn)]
                   + [functools.partial(emit_v, c0) for c0 in range(2 * attn_width, 3 * attn_width, tn)]
                   + [functools.partial(emit_gate_gelu, c0) for c0 in range(0, rec_width, tn)])
    n_gate_tiles = rec_width // gate_tile
    per = -(-len(matmul_work) // n_gate_tiles)
    for ct in range(n_gate_tiles):
        emit_recurrence_inputs(ct)
        for work in matmul_work[ct * per:(ct + 1) * per]:
            work()
    for work in matmul_work[n_gate_tiles * per:]:
        work()
    xbuf[0:pad, :] = xbuf[tm:tm + pad, :]


def _rope_tables(seq):
    inv_freq = ROPE_THETA ** (-jnp.arange(0, ROT_DIM, 2, dtype=F32) / ROT_DIM)
    ang = jnp.arange(seq, dtype=jnp.int32).astype(F32)[:, None] * inv_freq[None, :]
    cos, sin = jnp.cos(ang), jnp.sin(ang)
    ones = jnp.ones((seq, HEAD_DIM - ROT_DIM), F32)
    zeros = jnp.zeros((seq, HEAD_DIM - ROT_DIM), F32)
    cos_t = jnp.concatenate([cos, cos, ones], axis=1)
    sin_t = jnp.concatenate([-sin, sin, zeros], axis=1)
    return cos_t, sin_t


def _block_diag_gates(w_a, w_x, cw):
    nblk, bd, _ = w_a.shape
    per = cw // bd
    nt = nblk // per
    eye = jnp.eye(per, dtype=w_a.dtype)

    def bdiag(w):
        w = w.reshape(nt, per, bd, bd)
        full = jnp.einsum("tpij,pq->tpiqj", w, eye)
        return full.reshape(nt, cw, cw)

    return jnp.concatenate([bdiag(w_a), bdiag(w_x)], axis=2).astype(BF16)


def _in_proj(x2d, g, w_in, conv_w, conv_b, w_a, b_a, w_x, b_x, lam, seq, attn_width, rec_width,
             *, tm=256, tn=512, gate_tile=256):
    t, d = x2d.shape
    cols = w_in.shape[1]
    s_tiles = seq // tm
    cos_t, sin_t = _rope_tables(seq)
    wg = _block_diag_gates(w_a, w_x, gate_tile)
    kern = functools.partial(_in_proj_kernel, tn=tn, attn_width=attn_width, rec_width=rec_width,
                             gate_tile=gate_tile, s_tiles=s_tiles)
    whole = lambda shape: pl.BlockSpec(shape, lambda i: (0,) * len(shape))
    rec_row = lambda v: v.reshape(1, rec_width)
    rec_out = pl.BlockSpec((tm, rec_width), lambda i: (i, 0))
    rec_shape = jax.ShapeDtypeStruct((t, rec_width), F32)
    return pl.pallas_call(
        kern,
        grid=(t // tm,),
        in_specs=[
            pl.BlockSpec((tm, d), lambda i: (i, 0)),
            whole((1, d)),
            pl.BlockSpec((d, cols), lambda i: (0, 0), pipeline_mode=pl.Buffered(1)),
            pl.BlockSpec((tm, HEAD_DIM), lambda i: (i % s_tiles, 0)),
            pl.BlockSpec((tm, HEAD_DIM), lambda i: (i % s_tiles, 0)),
            whole((CONV_WIDTH, rec_width)),
            whole((1, rec_width)),
            whole(wg.shape),
            whole((1, rec_width)), whole((1, rec_width)), whole((1, rec_width)),
        ],
        out_specs=[pl.BlockSpec((tm, 3 * attn_width), lambda i: (i, 0)), rec_out, rec_out, rec_out],
        out_shape=[jax.ShapeDtypeStruct((t, 3 * attn_width), BF16), rec_shape, rec_shape, rec_shape],
        scratch_shapes=[pltpu.VMEM((tm + SUBLANES, rec_width), F32)],
        compiler_params=pltpu.CompilerParams(
            dimension_semantics=("arbitrary",), vmem_limit_bytes=VMEM_LIMIT),
        name="in_proj",
    )(x2d, g.reshape(1, d), w_in.astype(BF16), cos_t, sin_t, conv_w, rec_row(conv_b), wg,
      rec_row(b_a), rec_row(b_x), rec_row(lam))


def _moba_kernel(q_ref, k_ref, v_ref, o_ref, vt_scr, kmean_scr, s_scr, p_scr, *, nb):
    blk = MOBA_BLOCK
    scale_l2 = (HEAD_DIM ** -0.5) * LOG2E
    tb = (((1,), (1,)), ((), ()))

    ones_row = lax.broadcasted_iota(jnp.int32, (vt_scr.shape[0] - HEAD_DIM, vt_scr.shape[1]), 0) == 0
    vt_scr[HEAD_DIM:, :] = jnp.where(ones_row, 1.0, 0.0).astype(BF16)
    for n in range(nb):
        rows = slice(n * blk, (n + 1) * blk)
        vt_scr[:HEAD_DIM, rows] = v_ref[rows, :].astype(F32).T.astype(BF16)
        kmean_scr[n:n + 1, :] = jnp.mean(k_ref[rows, :].astype(F32), axis=0, keepdims=True)

    km = kmean_scr[...]
    km_hi = km.astype(BF16)
    km_lo = (km - km_hi.astype(F32)).astype(BF16)
    blk_id = lax.broadcasted_iota(jnp.int32, (nb, blk), 0)
    key_pos = lax.broadcasted_iota(jnp.int32, (blk, blk), 0)
    qry_pos = lax.broadcasted_iota(jnp.int32, (blk, blk), 1)
    causal = key_pos <= qry_pos

    def scores(qi):
        slot = qi % 2
        q = q_ref[qi * blk:(qi + 1) * blk, :]

        if qi <= MOBA_TOPK:
            bias = None
        else:
            gate = (lax.dot_general(km_hi, q, tb, preferred_element_type=F32)
                    + lax.dot_general(km_lo, q, tb, preferred_element_type=F32))
            eligible = blk_id < qi
            rem = jnp.where(eligible, gate, -jnp.inf)
            sel = jnp.zeros((nb, blk), jnp.bool_)
            for _t in range(MOBA_TOPK):
                mx = jnp.max(rem, axis=0, keepdims=True)
                first = jnp.min(jnp.where(rem == mx, blk_id, nb), axis=0, keepdims=True)
                pick = blk_id == first
                sel = jnp.logical_or(sel, pick)
                rem = jnp.where(pick, -jnp.inf, rem)
            bias = jnp.where(jnp.logical_and(sel, eligible), 0.0, NEG_BIG)

        m = None
        for n in range(qi + 1):
            rows = slice(n * blk, (n + 1) * blk)
            raw = lax.dot_general(k_ref[rows, :], q, tb, preferred_element_type=F32)
            if n == qi:
                raw = jnp.where(causal, raw, NEG_BIG)
            s_scr[slot, rows, :] = raw
            cm = jnp.max(raw, axis=0, keepdims=True) * scale_l2
            if bias is not None and n < qi:
                cm = cm + bias[n:n + 1, :]
            m = cm if m is None else jnp.maximum(m, cm)
        return bias, m

    def softmax_pv(qi, bias, m):
        slot = qi % 2
        nk = (qi + 1) * blk
        for n in range(qi + 1):
            rows = slice(n * blk, (n + 1) * blk)
            shift = -m
            if bias is not None and n < qi:
                shift = bias[n:n + 1, :] - m
            x = s_scr[slot, rows, :] * scale_l2 + shift
            p_scr[slot, rows, :] = jnp.exp2(x.astype(BF16))

        acc = jnp.dot(vt_scr[:, 0:nk], p_scr[slot, 0:nk, :], preferred_element_type=F32)
        l = acc[HEAD_DIM:HEAD_DIM + 1, :]
        o_ref[qi * blk:(qi + 1) * blk, :] = (acc[:HEAD_DIM, :] / l).T

    state = scores(0)
    for qi in range(nb):
        nxt = scores(qi + 1) if qi + 1 < nb else None
        softmax_pv(qi, *state)
        state = nxt


def _moba(qkv, batch, seq):
    t = qkv.shape[0]
    nb = seq // MOBA_BLOCK
    nh = N_ATTN_HEADS
    return pl.pallas_call(
        functools.partial(_moba_kernel, nb=nb),
        grid=(batch, nh),
        in_specs=[
            pl.BlockSpec((seq, HEAD_DIM), lambda b, h: (b, h)),
            pl.BlockSpec((seq, HEAD_DIM), lambda b, h: (b, nh + h)),
            pl.BlockSpec((seq, HEAD_DIM), lambda b, h: (b, 2 * nh + h)),
        ],
        out_specs=pl.BlockSpec((seq, HEAD_DIM), lambda b, h: (b, h)),
        out_shape=jax.ShapeDtypeStruct((t, nh * HEAD_DIM), F32),
        scratch_shapes=[
            pltpu.VMEM((HEAD_DIM + 2 * SUBLANES, seq), BF16),
            pltpu.VMEM((nb, HEAD_DIM), F32),
            pltpu.VMEM((2, seq, MOBA_BLOCK), F32),
            pltpu.VMEM((2, seq, MOBA_BLOCK), BF16),
        ],
        compiler_params=pltpu.CompilerParams(
            dimension_semantics=("parallel", "parallel"), vmem_limit_bytes=VMEM_LIMIT),
        name="moba",
    )(qkv, qkv, qkv)


def _rglru_kernel(a_ref, b_ref, gg_ref, o_ref, hcar, a_scr, b_scr, *, tl, cw):
    t = pl.program_id(2)

    @pl.when(t == 0)
    def _():
        hcar[...] = jnp.zeros_like(hcar)

    a = a_ref[...]
    b = b_ref[...]
    row8 = lax.broadcasted_iota(jnp.int32, (tl, cw), 0) & (SUBLANES - 1)
    for s in (1, 2, 4):
        ar = pltpu.roll(a, s, 0)
        br = pltpu.roll(b, s, 0)
        keep = row8 >= s
        b = jnp.where(keep, a * br + b, b)
        a = jnp.where(keep, a * ar, a)
    a_scr[...] = a
    b_scr[...] = b

    def group(gidx, carry):
        r0 = pl.multiple_of(gidx * SUBLANES, SUBLANES)
        hg = a_scr[pl.ds(r0, SUBLANES), :] * carry + b_scr[pl.ds(r0, SUBLANES), :]
        b_scr[pl.ds(r0, SUBLANES), :] = hg
        return hg[SUBLANES - 1:SUBLANES, :]

    hcar[...] = lax.fori_loop(0, tl // SUBLANES, group, hcar[...], unroll=8)
    o_ref[...] = b_scr[...] * gg_ref[...]


def _rglru(a, b, gg, batch, seq, *, tl=512, cw=512):
    t, r_width = a.shape
    nt = seq // tl
    blk = pl.BlockSpec((tl, cw), lambda bi, c, s: (bi * nt + s, c))
    return pl.pallas_call(
        functools.partial(_rglru_kernel, tl=tl, cw=cw),
        grid=(batch, r_width // cw, nt),
        in_specs=[blk, blk, blk],
        out_specs=blk,
        out_shape=jax.ShapeDtypeStruct((t, r_width), F32),
        scratch_shapes=[
            pltpu.VMEM((1, cw), F32),
            pltpu.VMEM((tl, cw), F32),
            pltpu.VMEM((tl, cw), F32),
        ],
        compiler_params=pltpu.CompilerParams(
            dimension_semantics=("parallel", "parallel", "arbitrary"), vmem_limit_bytes=VMEM_LIMIT),
        name="rglru",
    )(a, b, gg)


def _out_proj_kernel(x_ref, attn_ref, rec_ref, ga_ref, gr_ref, w_ref, pg_ref, o_ref):
    a = _rms(attn_ref[...], ga_ref[...]).astype(BF16)
    r = _rms(rec_ref[...], gr_ref[...]).astype(BF16)
    aw = a.shape[1]
    y = (jnp.dot(a, w_ref[:aw, :], preferred_element_type=F32)
         + jnp.dot(r, w_ref[aw:, :], preferred_element_type=F32))
    o_ref[...] = x_ref[...] + _rms(y, pg_ref[...])


def _out_proj(x2d, attn, rec, ga, gr, w_out, pg, *, tm=512):
    t, d = x2d.shape
    aw = attn.shape[1]
    rw = rec.shape[1]
    return pl.pallas_call(
        _out_proj_kernel,
        grid=(t // tm,),
        in_specs=[
            pl.BlockSpec((tm, d), lambda i: (i, 0)),
            pl.BlockSpec((tm, aw), lambda i: (i, 0)),
            pl.BlockSpec((tm, rw), lambda i: (i, 0)),
            pl.BlockSpec((1, aw), lambda i: (0, 0)),
            pl.BlockSpec((1, rw), lambda i: (0, 0)),
            pl.BlockSpec((aw + rw, d), lambda i: (0, 0)),
            pl.BlockSpec((1, d), lambda i: (0, 0)),
        ],
        out_specs=pl.BlockSpec((tm, d), lambda i: (i, 0)),
        out_shape=jax.ShapeDtypeStruct((t, d), F32),
        compiler_params=pltpu.CompilerParams(
            dimension_semantics=("parallel",), vmem_limit_bytes=VMEM_LIMIT),
        name="out_proj",
    )(x2d, attn, rec, ga.reshape(1, aw), gr.reshape(1, rw), w_out.astype(BF16), pg.reshape(1, d))


def kernel(x, ffn1_pre_g, ffn1_w_gate_up, ffn1_w_down, ffn1_post_g, mix_pre_g, w_in, conv_w, conv_b, rg_w_a, rg_b_a, rg_w_x, rg_b_x, rg_lambda, attn_out_g, rec_out_g, w_out, mix_post_g, ffn2_pre_g, ffn2_w_gate_up, ffn2_w_down, ffn2_post_g):
    batch, seq, d = x.shape
    depth = ffn1_pre_g.shape[0]
    attn_width = N_ATTN_HEADS * HEAD_DIM
    rec_width = conv_w.shape[-1]
    assert w_in.shape[-1] == 3 * attn_width + 2 * rec_width
    assert seq % 512 == 0 and seq % MOBA_BLOCK == 0
    h = x.reshape(batch * seq, d)
    for l in range(depth):
        h = _ffn(h, ffn1_pre_g[l], ffn1_w_gate_up[l], ffn1_w_down[l], ffn1_post_g[l])
        qkv, a, b, gg = _in_proj(h, mix_pre_g[l], w_in[l], conv_w[l], conv_b[l], rg_w_a[l], rg_b_a[l],
                                 rg_w_x[l], rg_b_x[l], rg_lambda[l], seq, attn_width, rec_width)
        attn = _moba(qkv, batch, seq)
        rec = _rglru(a, b, gg, batch, seq)
        h = _out_proj(h, attn, rec, attn_out_g[l], rec_out_g[l], w_out[l], mix_post_g[l])
        h = _ffn(h, ffn2_pre_g[l], ffn2_w_gate_up[l], ffn2_w_down[l], ffn2_post_g[l])
    return h.reshape(batch, seq, d)
```

```python
import functools

import jax
import jax.numpy as jnp
from jax import lax
from jax.experimental import pallas as pl
from jax.experimental.pallas import tpu as pltpu

F32 = jnp.float32
BF16 = jnp.bfloat16

N_ATTN_HEADS = 8
HEAD_DIM = 128
ROT_DIM = 32
ROPE_THETA = 500000.0
MOBA_BLOCK = 256
MOBA_TOPK = 3
CONV_WIDTH = 4
RGLRU_C = 8.0
FFN_HALF = 0.5
EPS = 1e-6

VMEM_LIMIT = 56 * 1024 * 1024
LANES = 128
SUBLANES = 8

NEG_BIG = -1e30
LOG2E = 1.4426950408889634


def _rms(x, g):
    ms = jnp.mean(x * x, axis=-1, keepdims=True)
    return x * lax.rsqrt(ms + EPS) * g


def _ffn_kernel(x_ref, preg_ref, wg0_ref, wg1_ref, wu0_ref, wu1_ref, wd0_ref, wd1_ref, postg_ref,
                o_ref, h_scr, *, n_first, row_blk):
    j = pl.program_id(1)
    last = pl.num_programs(1) - 1
    row_blocks = [slice(r, r + row_blk) for r in range(0, x_ref.shape[0], row_blk)]

    def weights(n_blocks):
        cat = lambda refs, axis: jnp.concatenate([r[...].astype(BF16) for r in refs], axis=axis)
        wgu = cat([wg0_ref, wg1_ref][:n_blocks] + [wu0_ref, wu1_ref][:n_blocks], 1)
        return wgu, cat([wd0_ref, wd1_ref][:n_blocks], 0)

    def swiglu_down(h, wgu, wdn):
        gu = jnp.dot(h, wgu, preferred_element_type=F32)
        half = gu.shape[1] // 2
        g = gu[:, :half]
        u = gu[:, half:]
        a = (g * jax.nn.sigmoid(g) * u).astype(BF16)
        return jnp.dot(a, wdn, preferred_element_type=F32)

    @pl.when(j == 0)
    def _():
        wgu, wdn = weights(n_first)
        for rows in row_blocks:
            h = _rms(x_ref[rows, :], preg_ref[...]).astype(BF16)
            h_scr[rows, :] = h
            o_ref[rows, :] = swiglu_down(h, wgu, wdn)

    @pl.when(jnp.logical_and(j > 0, j < last))
    def _():
        wgu, wdn = weights(2)
        o_ref[...] += swiglu_down(h_scr[...], wgu, wdn)

    @pl.when(j == last)
    def _():
        wgu, wdn = weights(2)
        for rows in row_blocks:
            acc = o_ref[rows, :] + swiglu_down(h_scr[rows, :], wgu, wdn)
            o_ref[rows, :] = x_ref[rows, :] + FFN_HALF * _rms(acc, postg_ref[...])


def _ffn(x2d, pre_g, w_gate_up, w_down, post_g, *, tm=1024, row_blk=256):
    t, d = x2d.shape
    f = w_down.shape[0]
    assert f % LANES == 0
    nblk = f // LANES
    n_steps = -(-nblk // 2)
    n_first = nblk - 2 * (n_steps - 1)
    assert n_steps >= 2
    blk_idx = lambda j, k: jnp.where(j == 0, jnp.minimum(k, n_first - 1), n_first + 2 * (j - 1) + k)
    col = lambda k, base: pl.BlockSpec((d, LANES), lambda i, j: (0, base + blk_idx(j, k)))
    row = lambda k: pl.BlockSpec((LANES, d), lambda i, j: (blk_idx(j, k), 0))
    return pl.pallas_call(
        functools.partial(_ffn_kernel, n_first=n_first, row_blk=row_blk),
        grid=(t // tm, n_steps),
        in_specs=[
            pl.BlockSpec((tm, d), lambda i, j: (i, 0)),
            pl.BlockSpec((1, d), lambda i, j: (0, 0)),
            col(0, 0), col(1, 0), col(0, nblk), col(1, nblk),
            row(0), row(1),
            pl.BlockSpec((1, d), lambda i, j: (0, 0)),
        ],
        out_specs=pl.BlockSpec((tm, d), lambda i, j: (i, 0)),
        out_shape=jax.ShapeDtypeStruct((t, d), F32),
        scratch_shapes=[pltpu.VMEM((tm, d), BF16)],
        compiler_params=pltpu.CompilerParams(
            dimension_semantics=("parallel", "arbitrary"), vmem_limit_bytes=VMEM_LIMIT),
        name="ffn",
    )(x2d, pre_g.reshape(1, d), w_gate_up, w_gate_up, w_gate_up, w_gate_up, w_down, w_down,
      post_g.reshape(1, d))


def _in_proj_kernel(x_ref, g_ref, w_ref, cos_ref, sin_ref, cw_ref, cb_ref, wg_ref, ba_ref, bx_ref,
                    lam_ref, qkv_ref, a_ref, b_ref, gg_ref, xbuf,
                    *, tn, attn_width, rec_width, gate_tile, s_tiles):
    i = pl.program_id(0)
    tm = x_ref.shape[0]
    pad = SUBLANES

    @pl.when(i % s_tiles == 0)
    def _():
        xbuf[0:pad, :] = jnp.zeros((pad, rec_width), F32)

    h = _rms(x_ref[...], g_ref[...]).astype(BF16)
    c = cos_ref[...]
    s = sin_ref[...]
    lane = lax.broadcasted_iota(jnp.int32, c.shape, 1)
    first_half = lane < ROT_DIM // 2

    def proj(col0):
        return jnp.dot(h, w_ref[:, col0:col0 + tn], preferred_element_type=F32)

    def emit_qk(col0):
        y = proj(col0)
        for hh in range(tn // HEAD_DIM):
            yh = y[:, hh * HEAD_DIM:(hh + 1) * HEAD_DIM]
            partner = jnp.where(first_half,
                                pltpu.roll(yh, HEAD_DIM - ROT_DIM // 2, 1),
                                pltpu.roll(yh, ROT_DIM // 2, 1))
            col = col0 + hh * HEAD_DIM
            qkv_ref[:, col:col + HEAD_DIM] = (yh * c + partner * s).astype(BF16)

    def emit_v(col0):
        qkv_ref[:, col0:col0 + tn] = proj(col0).astype(BF16)

    def emit_gate_gelu(col0):
        gg_ref[:, col0:col0 + tn] = jax.nn.gelu(proj(3 * attn_width + rec_width + col0))

    w = cw_ref[...]

    def emit_recurrence_inputs(ct):
        cols = slice(ct * gate_tile, (ct + 1) * gate_tile)
        xc = cb_ref[:, cols] + w[CONV_WIDTH - 1:CONV_WIDTH, cols] * xbuf[pad:pad + tm, cols]
        for j in range(1, CONV_WIDTH):
            xc = xc + w[CONV_WIDTH - 1 - j:CONV_WIDTH - j, cols] * xbuf[pad - j:pad - j + tm, cols]
        gates = jnp.dot(xc.astype(BF16), wg_ref[ct], preferred_element_type=F32)
        r = jax.nn.sigmoid(gates[:, :gate_tile] + ba_ref[:, cols])
        gi = jax.nn.sigmoid(gates[:, gate_tile:] + bx_ref[:, cols])
        z = -lam_ref[:, cols]
        softplus = jnp.maximum(z, 0.0) + jnp.log1p(jnp.exp(-jnp.abs(z)))
        log_a = (-RGLRU_C * r) * softplus
        a = jnp.exp(log_a)
        a_ref[:, cols] = a
        b_ref[:, cols] = jnp.sqrt(-jnp.tanh(log_a) * (a * a + 1.0)) * (gi * xc)

    for col0 in range(0, rec_width, tn):
        xbuf[pad:pad + tm, col0:col0 + tn] = proj(3 * attn_width + col0)
    matmul_work = ([functools.partial(emit_qk, c0) for c0 in range(0, 2 * attn_width, tn)]
                   + [functools.partial(emit_v, c0) for c0 in range(2 * attn_width, 3 * attn_width, tn)]
                   + [functools.partial(emit_gate_gelu, c0) for c0 in range(0, rec_width, tn)])
    n_gate_tiles = rec_width // gate_tile
    per = -(-len(matmul_work) // n_gate_tiles)
    for ct in range(n_gate_tiles):
        emit_recurrence_inputs(ct)
        for work in matmul_work[ct * per:(ct + 1) * per]:
            work()
    for work in matmul_work[n_gate_tiles * per:]:
        work()
    xbuf[0:pad, :] = xbuf[tm:tm + pad, :]


def _rope_tables(seq):
    inv_freq = ROPE_THETA ** (-jnp.arange(0, ROT_DIM, 2, dtype=F32) / ROT_DIM)
    ang = jnp.arange(seq, dtype=jnp.int32).astype(F32)[:, None] * inv_freq[None, :]
    cos, sin = jnp.cos(ang), jnp.sin(ang)
    ones = jnp.ones((seq, HEAD_DIM - ROT_DIM), F32)
    zeros = jnp.zeros((seq, HEAD_DIM - ROT_DIM), F32)
    cos_t = jnp.concatenate([cos, cos, ones], axis=1)
    sin_t = jnp.concatenate([-sin, sin, zeros], axis=1)
    return cos_t, sin_t


def _block_diag_gates(w_a, w_x, cw):
    nblk, bd, _ = w_a.shape
    per = cw // bd
    nt = nblk // per
    eye = jnp.eye(per, dtype=w_a.dtype)

    def bdiag(w):
        w = w.reshape(nt, per, bd, bd)
        full = jnp.einsum("tpij,pq->tpiqj", w, eye)
        return full.reshape(nt, cw, cw)

    return jnp.concatenate([bdiag(w_a), bdiag(w_x)], axis=2).astype(BF16)


def _in_proj(x2d, g, w_in, conv_w, conv_b, w_a, b_a, w_x, b_x, lam, seq, attn_width, rec_width,
             *, tm=256, tn=512, gate_tile=256):
    t, d = x2d.shape
    cols = w_in.shape[1]
    s_tiles = seq // tm
    cos_t, sin_t = _rope_tables(seq)
    wg = _block_diag_gates(w_a, w_x, gate_tile)
    kern = functools.partial(_in_proj_kernel, tn=tn, attn_width=attn_width, rec_width=rec_width,
                             gate_tile=gate_tile, s_tiles=s_tiles)
    whole = lambda shape: pl.BlockSpec(shape, lambda i: (0,) * len(shape))
    rec_row = lambda v: v.reshape(1, rec_width)
    rec_out = pl.BlockSpec((tm, rec_width), lambda i: (i, 0))
    rec_shape = jax.ShapeDtypeStruct((t, rec_width), F32)
    return pl.pallas_call(
        kern,
        grid=(t // tm,),
        in_specs=[
            pl.BlockSpec((tm, d), lambda i: (i, 0)),
            whole((1, d)),
            pl.BlockSpec((d, cols), lambda i: (0, 0), pipeline_mode=pl.Buffered(1)),
            pl.BlockSpec((tm, HEAD_DIM), lambda i: (i % s_tiles, 0)),
            pl.BlockSpec((tm, HEAD_DIM), lambda i: (i % s_tiles, 0)),
            whole((CONV_WIDTH, rec_width)),
            whole((1, rec_width)),
            whole(wg.shape),
            whole((1, rec_width)), whole((1, rec_width)), whole((1, rec_width)),
        ],
        out_specs=[pl.BlockSpec((tm, 3 * attn_width), lambda i: (i, 0)), rec_out, rec_out, rec_out],
        out_shape=[jax.ShapeDtypeStruct((t, 3 * attn_width), BF16), rec_shape, rec_shape, rec_shape],
        scratch_shapes=[pltpu.VMEM((tm + SUBLANES, rec_width), F32)],
        compiler_params=pltpu.CompilerParams(
            dimension_semantics=("arbitrary",), vmem_limit_bytes=VMEM_LIMIT),
        name="in_proj",
    )(x2d, g.reshape(1, d), w_in.astype(BF16), cos_t, sin_t, conv_w, rec_row(conv_b), wg,
      rec_row(b_a), rec_row(b_x), rec_row(lam))


def _moba_kernel(q_ref, k_ref, v_ref, o_ref, vt_scr, kmean_scr, s_scr, p_scr, *, nb):
    blk = MOBA_BLOCK
    scale_l2 = (HEAD_DIM ** -0.5) * LOG2E
    tb = (((1,), (1,)), ((), ()))

    ones_row = lax.broadcasted_iota(jnp.int32, (vt_scr.shape[0] - HEAD_DIM, vt_scr.shape[1]), 0) == 0
    vt_scr[HEAD_DIM:, :] = jnp.where(ones_row, 1.0, 0.0).astype(BF16)
    for n in range(nb):
        rows = slice(n * blk, (n + 1) * blk)
        vt_scr[:HEAD_DIM, rows] = v_ref[rows, :].astype(F32).T.astype(BF16)
        kmean_scr[n:n + 1, :] = jnp.mean(k_ref[rows, :].astype(F32), axis=0, keepdims=True)

    km = kmean_scr[...]
    km_hi = km.astype(BF16)
    km_lo = (km - km_hi.astype(F32)).astype(BF16)
    blk_id = lax.broadcasted_iota(jnp.int32, (nb, blk), 0)
    key_pos = lax.broadcasted_iota(jnp.int32, (blk, blk), 0)
    qry_pos = lax.broadcasted_iota(jnp.int32, (blk, blk), 1)
    causal = key_pos <= qry_pos

    def scores(qi):
        slot = qi % 2
        q = q_ref[qi * blk:(qi + 1) * blk, :]

        if qi <= MOBA_TOPK:
            bias = None
        else:
            gate = (lax.dot_general(km_hi, q, tb, preferred_element_type=F32)
                    + lax.dot_general(km_lo, q, tb, preferred_element_type=F32))
            eligible = blk_id < qi
            rem = jnp.where(eligible, gate, -jnp.inf)
            sel = jnp.zeros((nb, blk), jnp.bool_)
            for _t in range(MOBA_TOPK):
                mx = jnp.max(rem, axis=0, keepdims=True)
                first = jnp.min(jnp.where(rem == mx, blk_id, nb), axis=0, keepdims=True)
                pick = blk_id == first
                sel = jnp.logical_or(sel, pick)
                rem = jnp.where(pick, -jnp.inf, rem)
            bias = jnp.where(jnp.logical_and(sel, eligible), 0.0, NEG_BIG)

        m = None
        for n in range(qi + 1):
            rows = slice(n * blk, (n + 1) * blk)
            raw = lax.dot_general(k_ref[rows, :], q, tb, preferred_element_type=F32)
            if n == qi:
                raw = jnp.where(causal, raw, NEG_BIG)
            s_scr[slot, rows, :] = raw
            cm = jnp.max(raw, axis=0, keepdims=True) * scale_l2
            if bias is not None and n < qi:
                cm = cm + bias[n:n + 1, :]
            m = cm if m is None else jnp.maximum(m, cm)
        return bias, m

    def softmax_pv(qi, bias, m):
        slot = qi % 2
        nk = (qi + 1) * blk
        for n in range(qi + 1):
            rows = slice(n * blk, (n + 1) * blk)
            shift = -m
            if bias is not None and n < qi:
                shift = bias[n:n + 1, :] - m
            x = s_scr[slot, rows, :] * scale_l2 + shift
            p_scr[slot, rows, :] = jnp.exp2(x.astype(BF16))

        acc = jnp.dot(vt_scr[:, 0:nk], p_scr[slot, 0:nk, :], preferred_element_type=F32)
        l = acc[HEAD_DIM:HEAD_DIM + 1, :]
        o_ref[qi * blk:(qi + 1) * blk, :] = (acc[:HEAD_DIM, :] / l).T

    state = scores(0)
    for qi in range(nb):
        nxt = scores(qi + 1) if qi + 1 < nb else None
        softmax_pv(qi, *state)
        state = nxt


def _moba(qkv, batch, seq):
    t = qkv.shape[0]
    nb = seq // MOBA_BLOCK
    nh = N_ATTN_HEADS
    return pl.pallas_call(
        functools.partial(_moba_kernel, nb=nb),
        grid=(batch, nh),
        in_specs=[
            pl.BlockSpec((seq, HEAD_DIM), lambda b, h: (b, h)),
            pl.BlockSpec((seq, HEAD_DIM), lambda b, h: (b, nh + h)),
            pl.BlockSpec((seq, HEAD_DIM), lambda b, h: (b, 2 * nh + h)),
        ],
        out_specs=pl.BlockSpec((seq, HEAD_DIM), lambda b, h: (b, h)),
        out_shape=jax.ShapeDtypeStruct((t, nh * HEAD_DIM), F32),
        scratch_shapes=[
            pltpu.VMEM((HEAD_DIM + 2 * SUBLANES, seq), BF16),
            pltpu.VMEM((nb, HEAD_DIM), F32),
            pltpu.VMEM((2, seq, MOBA_BLOCK), F32),
            pltpu.VMEM((2, seq, MOBA_BLOCK), BF16),
        ],
        compiler_params=pltpu.CompilerParams(
            dimension_semantics=("parallel", "parallel"), vmem_limit_bytes=VMEM_LIMIT),
        name="moba",
    )(qkv, qkv, qkv)


def _out_proj_kernel(x_ref, attn_ref, a_ref, b_ref, gg_ref, ga_ref, gr_ref, w_ref, pg_ref, o_ref, hcar,
                     *, s_tiles):
    i = pl.program_id(0)
    tm = x_ref.shape[0]

    @pl.when(i % s_tiles == 0)
    def _():
        hcar[...] = jnp.zeros_like(hcar)

    a = a_ref[...]
    b = b_ref[...]
    row8 = lax.broadcasted_iota(jnp.int32, a.shape, 0) & (SUBLANES - 1)
    for s in (1, 2, 4):
        ar = pltpu.roll(a, s, 0)
        br = pltpu.roll(b, s, 0)
        keep = row8 >= s
        b = jnp.where(keep, a * br + b, b)
        a = jnp.where(keep, a * ar, a)
    carry = hcar[...]
    groups = []
    for r0 in range(0, tm, SUBLANES):
        hg = a[r0:r0 + SUBLANES, :] * carry + b[r0:r0 + SUBLANES, :]
        groups.append(hg)
        carry = hg[SUBLANES - 1:SUBLANES, :]
    hcar[...] = carry
    rec = jnp.concatenate(groups, axis=0) * gg_ref[...]

    at = _rms(attn_ref[...], ga_ref[...]).astype(BF16)
    rc = _rms(rec, gr_ref[...]).astype(BF16)
    aw = at.shape[1]
    y = (jnp.dot(at, w_ref[:aw, :], preferred_element_type=F32)
         + jnp.dot(rc, w_ref[aw:, :], preferred_element_type=F32))
    o_ref[...] = x_ref[...] + _rms(y, pg_ref[...])


def _out_proj(x2d, attn, a, b, gg, ga, gr, w_out, pg, seq, *, tm=512):
    t, d = x2d.shape
    aw = attn.shape[1]
    rw = a.shape[1]
    rec_blk = pl.BlockSpec((tm, rw), lambda i: (i, 0))
    return pl.pallas_call(
        functools.partial(_out_proj_kernel, s_tiles=seq // tm),
        grid=(t // tm,),
        in_specs=[
            pl.BlockSpec((tm, d), lambda i: (i, 0)),
            pl.BlockSpec((tm, aw), lambda i: (i, 0)),
            rec_blk, rec_blk, rec_blk,
            pl.BlockSpec((1, aw), lambda i: (0, 0)),
            pl.BlockSpec((1, rw), lambda i: (0, 0)),
            pl.BlockSpec((aw + rw, d), lambda i: (0, 0), pipeline_mode=pl.Buffered(1)),
            pl.BlockSpec((1, d), lambda i: (0, 0)),
        ],
        out_specs=pl.BlockSpec((tm, d), lambda i: (i, 0)),
        out_shape=jax.ShapeDtypeStruct((t, d), F32),
        scratch_shapes=[pltpu.VMEM((1, rw), F32)],
        compiler_params=pltpu.CompilerParams(
            dimension_semantics=("arbitrary",), vmem_limit_bytes=VMEM_LIMIT),
        name="out_proj",
    )(x2d, attn, a, b, gg, ga.reshape(1, aw), gr.reshape(1, rw), w_out.astype(BF16), pg.reshape(1, d))


def kernel(x, ffn1_pre_g, ffn1_w_gate_up, ffn1_w_down, ffn1_post_g, mix_pre_g, w_in, conv_w, conv_b, rg_w_a, rg_b_a, rg_w_x, rg_b_x, rg_lambda, attn_out_g, rec_out_g, w_out, mix_post_g, ffn2_pre_g, ffn2_w_gate_up, ffn2_w_down, ffn2_post_g):
    batch, seq, d = x.shape
    depth = ffn1_pre_g.shape[0]
    attn_width = N_ATTN_HEADS * HEAD_DIM
    rec_width = conv_w.shape[-1]
    assert w_in.shape[-1] == 3 * attn_width + 2 * rec_width
    assert seq % 512 == 0 and seq % MOBA_BLOCK == 0
    h = x.reshape(batch * seq, d)
    for l in range(depth):
        h = _ffn(h, ffn1_pre_g[l], ffn1_w_gate_up[l], ffn1_w_down[l], ffn1_post_g[l])
        qkv, a, b, gg = _in_proj(h, mix_pre_g[l], w_in[l], conv_w[l], conv_b[l], rg_w_a[l], rg_b_a[l],
                                 rg_w_x[l], rg_b_x[l], rg_lambda[l], seq, attn_width, rec_width)
        attn = _moba(qkv, batch, seq)
        h = _out_proj(h, attn, a, b, gg, attn_out_g[l], rec_out_g[l], w_out[l], mix_post_g[l], seq)
        h = _ffn(h, ffn2_pre_g[l], ffn2_w_gate_up[l], ffn2_w_down[l], ffn2_post_g[l])
    return h.reshape(batch, seq, d)
```

```python
import functools

import jax
import jax.numpy as jnp
from jax import lax
from jax.experimental import pallas as pl
from jax.experimental.pallas import tpu as pltpu

F32 = jnp.float32
BF16 = jnp.bfloat16

N_ATTN_HEADS = 8
HEAD_DIM = 128
ROT_DIM = 32
ROPE_THETA = 500000.0
MOBA_BLOCK = 256
MOBA_TOPK = 3
CONV_WIDTH = 4
RGLRU_C = 8.0
FFN_HALF = 0.5
EPS = 1e-6

VMEM_LIMIT = 56 * 1024 * 1024
LANES = 128
SUBLANES = 8

NEG_BIG = -1e30
LOG2E = 1.4426950408889634


def _rms(x, g):
    ms = jnp.mean(x * x, axis=-1, keepdims=True)
    return x * lax.rsqrt(ms + EPS) * g


def _ffn_steps(j, last, x_ref, preg_ref, postg_ref, o_ref, h_scr, weights, row_blk):
    row_blocks = [slice(r, r + row_blk) for r in range(0, x_ref.shape[0], row_blk)]

    def swiglu_down(h, w):
        wg, wu, wd = w
        g = jnp.dot(h, wg, preferred_element_type=F32)
        u = jnp.dot(h, wu, preferred_element_type=F32)
        a = (g * jax.nn.sigmoid(g) * u).astype(BF16)
        return jnp.dot(a, wd, preferred_element_type=F32)

    @pl.when(j == 0)
    def _():
        w = weights(False)
        for rows in row_blocks:
            h = _rms(x_ref[rows, :], preg_ref[...]).astype(BF16)
            h_scr[rows, :] = h
            o_ref[rows, :] = swiglu_down(h, w)

    @pl.when(jnp.logical_and(j > 0, j < last))
    def _():
        o_ref[...] += swiglu_down(h_scr[...], weights(False))

    @pl.when(j == last)
    def _():
        w = weights(True)
        for rows in row_blocks:
            acc = o_ref[rows, :] + swiglu_down(h_scr[rows, :], w)
            o_ref[rows, :] = x_ref[rows, :] + FFN_HALF * _rms(acc, postg_ref[...])


def _ffn_head_kernel(x_ref, preg_ref, wg0_ref, wg1_ref, wu0_ref, wu1_ref, wd0_ref, wd1_ref, postg_ref,
                     o_ref, cg_ref, cu_ref, cd_ref, h_scr, *, n_last, row_blk):
    j = pl.program_id(0)

    def weights(is_last):
        n = n_last if is_last else 2
        pick = lambda refs, axis: jnp.concatenate(
            [r[...].astype(BF16) if k < n else jnp.zeros(r.shape, BF16) for k, r in enumerate(refs)],
            axis=axis)
        w = pick([wg0_ref, wg1_ref], 1), pick([wu0_ref, wu1_ref], 1), pick([wd0_ref, wd1_ref], 0)
        cg_ref[...], cu_ref[...], cd_ref[...] = w
        return w

    _ffn_steps(j, pl.num_programs(0) - 1, x_ref, preg_ref, postg_ref, o_ref, h_scr, weights, row_blk)


def _ffn_tail_kernel(x_ref, preg_ref, wg_ref, wu_ref, wd_ref, postg_ref, head_out_ref, o_ref, h_scr,
                     *, row_blk):
    del head_out_ref
    weights = lambda is_last: (wg_ref[...], wu_ref[...], wd_ref[...])
    _ffn_steps(pl.program_id(1), pl.num_programs(1) - 1, x_ref, preg_ref, postg_ref, o_ref, h_scr,
               weights, row_blk)


def _ffn(x2d, pre_g, w_gate_up, w_down, post_g, *, tm=1024, tf=512, row_blk=256):
    t, d = x2d.shape
    f = w_down.shape[0]
    assert f % LANES == 0 and tf % (2 * LANES) == 0
    nblk = f // LANES
    fp = -(-f // tf) * tf
    n_head = fp // (2 * LANES)
    n_last = nblk - 2 * (n_head - 1)
    assert 1 <= n_last <= 2
    pre = pre_g.reshape(1, d)
    post = post_g.reshape(1, d)
    vec = pl.BlockSpec((1, d), lambda *idx: (0, 0))
    col = lambda k, base: pl.BlockSpec(
        (d, LANES), lambda j: (0, base + jnp.minimum(2 * j + k, nblk - 1)))
    row = lambda k: pl.BlockSpec((LANES, d), lambda j: (jnp.minimum(2 * j + k, nblk - 1), 0))
    once = pl.Buffered(1)
    head_out, cg, cu, cd = pl.pallas_call(
        functools.partial(_ffn_head_kernel, n_last=n_last, row_blk=row_blk),
        grid=(n_head,),
        in_specs=[
            pl.BlockSpec((tm, d), lambda j: (0, 0), pipeline_mode=once),
            vec,
            col(0, 0), col(1, 0), col(0, nblk), col(1, nblk),
            row(0), row(1),
            vec,
        ],
        out_specs=[
            pl.BlockSpec((tm, d), lambda j: (0, 0)),
            pl.BlockSpec((d, 2 * LANES), lambda j: (0, j)),
            pl.BlockSpec((d, 2 * LANES), lambda j: (0, j)),
            pl.BlockSpec((2 * LANES, d), lambda j: (j, 0)),
        ],
        out_shape=[
            jax.ShapeDtypeStruct((t, d), F32),
            jax.ShapeDtypeStruct((d, fp), BF16),
            jax.ShapeDtypeStruct((d, fp), BF16),
            jax.ShapeDtypeStruct((fp, d), BF16),
        ],
        scratch_shapes=[pltpu.VMEM((tm, d), BF16)],
        compiler_params=pltpu.CompilerParams(
            dimension_semantics=("arbitrary",), vmem_limit_bytes=VMEM_LIMIT),
        name="ffn_head",
    )(x2d, pre, w_gate_up, w_gate_up, w_gate_up, w_gate_up, w_down, w_down, post)
    if t == tm:
        return head_out
    return pl.pallas_call(
        functools.partial(_ffn_tail_kernel, row_blk=row_blk),
        grid=(t // tm - 1, fp // tf),
        in_specs=[
            pl.BlockSpec((tm, d), lambda i, j: (i + 1, 0)),
            vec,
            pl.BlockSpec((d, tf), lambda i, j: (0, j)),
            pl.BlockSpec((d, tf), lambda i, j: (0, j)),
            pl.BlockSpec((tf, d), lambda i, j: (j, 0)),
            vec,
            pl.BlockSpec(memory_space=pl.ANY),
        ],
        out_specs=pl.BlockSpec((tm, d), lambda i, j: (i + 1, 0)),
        out_shape=jax.ShapeDtypeStruct((t, d), F32),
        input_output_aliases={6: 0},
        scratch_shapes=[pltpu.VMEM((tm, d), BF16)],
        compiler_params=pltpu.CompilerParams(
            dimension_semantics=("parallel", "arbitrary"), vmem_limit_bytes=VMEM_LIMIT),
        name="ffn_tail",
    )(x2d, pre, cg, cu, cd, post, head_out)


def _in_proj_kernel(x_ref, g_ref, w_ref, cos_ref, sin_ref, cw_ref, cb_ref, wg_ref, ba_ref, bx_ref,
                    lam_ref, qkv_ref, a_ref, b_ref, gg_ref, xbuf,
                    *, tn, attn_width, rec_width, gate_tile, s_tiles):
    i = pl.program_id(0)
    tm = x_ref.shape[0]
    pad = SUBLANES

    @pl.when(i % s_tiles == 0)
    def _():
        xbuf[0:pad, :] = jnp.zeros((pad, rec_width), F32)

    h = _rms(x_ref[...], g_ref[...]).astype(BF16)
    c = cos_ref[...]
    s = sin_ref[...]
    lane = lax.broadcasted_iota(jnp.int32, c.shape, 1)
    first_half = lane < ROT_DIM // 2

    def proj(col0):
        return jnp.dot(h, w_ref[:, col0:col0 + tn], preferred_element_type=F32)

    def emit_qk(col0):
        y = proj(col0)
        for hh in range(tn // HEAD_DIM):
            yh = y[:, hh * HEAD_DIM:(hh + 1) * HEAD_DIM]
            partner = jnp.where(first_half,
                                pltpu.roll(yh, HEAD_DIM - ROT_DIM // 2, 1),
                                pltpu.roll(yh, ROT_DIM // 2, 1))
            col = col0 + hh * HEAD_DIM
            qkv_ref[:, col:col + HEAD_DIM] = (yh * c + partner * s).astype(BF16)

    def emit_v(col0):
        qkv_ref[:, col0:col0 + tn] = proj(col0).astype(BF16)

    def emit_gate_gelu(col0):
        gg_ref[:, col0:col0 + tn] = jax.nn.gelu(proj(3 * attn_width + rec_width + col0))

    w = cw_ref[...]

    def emit_recurrence_inputs(ct):
        cols = slice(ct * gate_tile, (ct + 1) * gate_tile)
        xc = cb_ref[:, cols] + w[CONV_WIDTH - 1:CONV_WIDTH, cols] * xbuf[pad:pad + tm, cols]
        for j in range(1, CONV_WIDTH):
            xc = xc + w[CONV_WIDTH - 1 - j:CONV_WIDTH - j, cols] * xbuf[pad - j:pad - j + tm, cols]
        gates = jnp.dot(xc.astype(BF16), wg_ref[ct], preferred_element_type=F32)
        r = jax.nn.sigmoid(gates[:, :gate_tile] + ba_ref[:, cols])
        gi = jax.nn.sigmoid(gates[:, gate_tile:] + bx_ref[:, cols])
        z = -lam_ref[:, cols]
        softplus = jnp.maximum(z, 0.0) + jnp.log1p(jnp.exp(-jnp.abs(z)))
        log_a = (-RGLRU_C * r) * softplus
        a = jnp.exp(log_a)
        a_ref[:, cols] = a
        b_ref[:, cols] = jnp.sqrt(-jnp.tanh(log_a) * (a * a + 1.0)) * (gi * xc)

    for col0 in range(0, rec_width, tn):
        xbuf[pad:pad + tm, col0:col0 + tn] = proj(3 * attn_width + col0)
    matmul_work = ([functools.partial(emit_qk, c0) for c0 in range(0, 2 * attn_width, tn)]
                   + [functools.partial(emit_v, c0) for c0 in range(2 * attn_width, 3 * attn_width, tn)]
                   + [functools.partial(emit_gate_gelu, c0) for c0 in range(0, rec_width, tn)])
    n_gate_tiles = rec_width // gate_tile
    per = -(-len(matmul_work) // n_gate_tiles)
    for ct in range(n_gate_tiles):
        emit_recurrence_inputs(ct)
        for work in matmul_work[ct * per:(ct + 1) * per]:
            work()
    for work in matmul_work[n_gate_tiles * per:]:
        work()
    xbuf[0:pad, :] = xbuf[tm:tm + pad, :]


def _rope_tables(seq):
    inv_freq = ROPE_THETA ** (-jnp.arange(0, ROT_DIM, 2, dtype=F32) / ROT_DIM)
    ang = jnp.arange(seq, dtype=jnp.int32).astype(F32)[:, None] * inv_freq[None, :]
    cos, sin = jnp.cos(ang), jnp.sin(ang)
    ones = jnp.ones((seq, HEAD_DIM - ROT_DIM), F32)
    zeros = jnp.zeros((seq, HEAD_DIM - ROT_DIM), F32)
    cos_t = jnp.concatenate([cos, cos, ones], axis=1)
    sin_t = jnp.concatenate([-sin, sin, zeros], axis=1)
    return cos_t, sin_t


def _block_diag_gates(w_a, w_x, cw):
    nblk, bd, _ = w_a.shape
    per = cw // bd
    nt = nblk // per
    eye = jnp.eye(per, dtype=w_a.dtype)

    def bdiag(w):
        w = w.reshape(nt, per, bd, bd)
        full = jnp.einsum("tpij,pq->tpiqj", w, eye)
        return full.reshape(nt, cw, cw)

    return jnp.concatenate([bdiag(w_a), bdiag(w_x)], axis=2).astype(BF16)


def _in_proj(x2d, g, w_in, conv_w, conv_b, w_a, b_a, w_x, b_x, lam, seq, attn_width, rec_width,
             *, tm=256, tn=512, gate_tile=256):
    t, d = x2d.shape
    cols = w_in.shape[1]
    s_tiles = seq // tm
    cos_t, sin_t = _rope_tables(seq)
    wg = _block_diag_gates(w_a, w_x, gate_tile)
    kern = functools.partial(_in_proj_kernel, tn=tn, attn_width=attn_width, rec_width=rec_width,
                             gate_tile=gate_tile, s_tiles=s_tiles)
    whole = lambda shape: pl.BlockSpec(shape, lambda i: (0,) * len(shape))
    rec_row = lambda v: v.reshape(1, rec_width)
    rec_out = pl.BlockSpec((tm, rec_width), lambda i: (i, 0))
    rec_shape = jax.ShapeDtypeStruct((t, rec_width), F32)
    return pl.pallas_call(
        kern,
        grid=(t // tm,),
        in_specs=[
            pl.BlockSpec((tm, d), lambda i: (i, 0)),
            whole((1, d)),
            pl.BlockSpec((d, cols), lambda i: (0, 0), pipeline_mode=pl.Buffered(1)),
            pl.BlockSpec((tm, HEAD_DIM), lambda i: (i % s_tiles, 0)),
            pl.BlockSpec((tm, HEAD_DIM), lambda i: (i % s_tiles, 0)),
            whole((CONV_WIDTH, rec_width)),
            whole((1, rec_width)),
            whole(wg.shape),
            whole((1, rec_width)), whole((1, rec_width)), whole((1, rec_width)),
        ],
        out_specs=[pl.BlockSpec((tm, 3 * attn_width), lambda i: (i, 0)), rec_out, rec_out, rec_out],
        out_shape=[jax.ShapeDtypeStruct((t, 3 * attn_width), BF16), rec_shape, rec_shape, rec_shape],
        scratch_shapes=[pltpu.VMEM((tm + SUBLANES, rec_width), F32)],
        compiler_params=pltpu.CompilerParams(
            dimension_semantics=("arbitrary",), vmem_limit_bytes=VMEM_LIMIT),
        name="in_proj",
    )(x2d, g.reshape(1, d), w_in.astype(BF16), cos_t, sin_t, conv_w, rec_row(conv_b), wg,
      rec_row(b_a), rec_row(b_x), rec_row(lam))


def _moba_kernel(q_ref, k_ref, v_ref, o_ref, vt_scr, kmean_scr, s_scr, p_scr, *, nb):
    blk = MOBA_BLOCK
    scale_l2 = (HEAD_DIM ** -0.5) * LOG2E
    tb = (((1,), (1,)), ((), ()))

    ones_row = lax.broadcasted_iota(jnp.int32, (vt_scr.shape[0] - HEAD_DIM, vt_scr.shape[1]), 0) == 0
    vt_scr[HEAD_DIM:, :] = jnp.where(ones_row, 1.0, 0.0).astype(BF16)
    for n in range(nb):
        rows = slice(n * blk, (n + 1) * blk)
        vt_scr[:HEAD_DIM, rows] = v_ref[rows, :].astype(F32).T.astype(BF16)
        kmean_scr[n:n + 1, :] = jnp.mean(k_ref[rows, :].astype(F32), axis=0, keepdims=True)

    km = kmean_scr[...]
    km_hi = km.astype(BF16)
    km_lo = (km - km_hi.astype(F32)).astype(BF16)
    blk_id = lax.broadcasted_iota(jnp.int32, (nb, blk), 0)
    key_pos = lax.broadcasted_iota(jnp.int32, (blk, blk), 0)
    qry_pos = lax.broadcasted_iota(jnp.int32, (blk, blk), 1)
    causal = key_pos <= qry_pos

    def scores(qi):
        slot = qi % 2
        q = q_ref[qi * blk:(qi + 1) * blk, :]

        if qi <= MOBA_TOPK:
            bias = None
        else:
            gate = (lax.dot_general(km_hi, q, tb, preferred_element_type=F32)
                    + lax.dot_general(km_lo, q, tb, preferred_element_type=F32))
            eligible = blk_id < qi
            rem = jnp.where(eligible, gate, -jnp.inf)
            sel = jnp.zeros((nb, blk), jnp.bool_)
            for _t in range(MOBA_TOPK):
                mx = jnp.max(rem, axis=0, keepdims=True)
                first = jnp.min(jnp.where(rem == mx, blk_id, nb), axis=0, keepdims=True)
                pick = blk_id == first
                sel = jnp.logical_or(sel, pick)
                rem = jnp.where(pick, -jnp.inf, rem)
            bias = jnp.where(jnp.logical_and(sel, eligible), 0.0, NEG_BIG)

        m = None
        for n in range(qi + 1):
            rows = slice(n * blk, (n + 1) * blk)
            raw = lax.dot_general(k_ref[rows, :], q, tb, preferred_element_type=F32)
            if n == qi:
                raw = jnp.where(causal, raw, NEG_BIG)
            s_scr[slot, rows, :] = raw
            cm = jnp.max(raw, axis=0, keepdims=True) * scale_l2
            if bias is not None and n < qi:
                cm = cm + bias[n:n + 1, :]
            m = cm if m is None else jnp.maximum(m, cm)
        return bias, m

    def softmax_pv(qi, bias, m):
        slot = qi % 2
        nk = (qi + 1) * blk
        for n in range(qi + 1):
            rows = slice(n * blk, (n + 1) * blk)
            shift = -m
            if bias is not None and n < qi:
                shift = bias[n:n + 1, :] - m
            x = s_scr[slot, rows, :] * scale_l2 + shift
            p_scr[slot, rows, :] = jnp.exp2(x.astype(BF16))

        acc = jnp.dot(vt_scr[:, 0:nk], p_scr[slot, 0:nk, :], preferred_element_type=F32)
        l = acc[HEAD_DIM:HEAD_DIM + 1, :]
        o_ref[qi * blk:(qi + 1) * blk, :] = (acc[:HEAD_DIM, :] / l).T

    state = scores(0)
    for qi in range(nb):
        nxt = scores(qi + 1) if qi + 1 < nb else None
        softmax_pv(qi, *state)
        state = nxt


def _moba(qkv, batch, seq):
    t = qkv.shape[0]
    nb = seq // MOBA_BLOCK
    nh = N_ATTN_HEADS
    return pl.pallas_call(
        functools.partial(_moba_kernel, nb=nb),
        grid=(batch, nh),
        in_specs=[
            pl.BlockSpec((seq, HEAD_DIM), lambda b, h: (b, h)),
            pl.BlockSpec((seq, HEAD_DIM), lambda b, h: (b, nh + h)),
            pl.BlockSpec((seq, HEAD_DIM), lambda b, h: (b, 2 * nh + h)),
        ],
        out_specs=pl.BlockSpec((seq, HEAD_DIM), lambda b, h: (b, h)),
        out_shape=jax.ShapeDtypeStruct((t, nh * HEAD_DIM), F32),
        scratch_shapes=[
            pltpu.VMEM((HEAD_DIM + 2 * SUBLANES, seq), BF16),
            pltpu.VMEM((nb, HEAD_DIM), F32),
            pltpu.VMEM((2, seq, MOBA_BLOCK), F32),
            pltpu.VMEM((2, seq, MOBA_BLOCK), BF16),
        ],
        compiler_params=pltpu.CompilerParams(
            dimension_semantics=("parallel", "parallel"), vmem_limit_bytes=VMEM_LIMIT),
        name="moba",
    )(qkv, qkv, qkv)


def _out_proj_kernel(x_ref, attn_ref, a_ref, b_ref, gg_ref, ga_ref, gr_ref, w_ref, pg_ref, o_ref, hcar,
                     *, s_tiles):
    i = pl.program_id(0)
    tm = x_ref.shape[0]

    @pl.when(i % s_tiles == 0)
    def _():
        hcar[...] = jnp.zeros_like(hcar)

    a = a_ref[...]
    b = b_ref[...]
    row8 = lax.broadcasted_iota(jnp.int32, a.shape, 0) & (SUBLANES - 1)
    for s in (1, 2, 4):
        ar = pltpu.roll(a, s, 0)
        br = pltpu.roll(b, s, 0)
        keep = row8 >= s
        b = jnp.where(keep, a * br + b, b)
        a = jnp.where(keep, a * ar, a)
    carry = hcar[...]
    groups = []
    for r0 in range(0, tm, SUBLANES):
        hg = a[r0:r0 + SUBLANES, :] * carry + b[r0:r0 + SUBLANES, :]
        groups.append(hg)
        carry = hg[SUBLANES - 1:SUBLANES, :]
    hcar[...] = carry
    rec = jnp.concatenate(groups, axis=0) * gg_ref[...]

    at = _rms(attn_ref[...], ga_ref[...]).astype(BF16)
    rc = _rms(rec, gr_ref[...]).astype(BF16)
    aw = at.shape[1]
    y = (jnp.dot(at, w_ref[:aw, :], preferred_element_type=F32)
         + jnp.dot(rc, w_ref[aw:, :], preferred_element_type=F32))
    o_ref[...] = x_ref[...] + _rms(y, pg_ref[...])


def _out_proj(x2d, attn, a, b, gg, ga, gr, w_out, pg, seq, *, tm=512):
    t, d = x2d.shape
    aw = attn.shape[1]
    rw = a.shape[1]
    rec_blk = pl.BlockSpec((tm, rw), lambda i: (i, 0))
    return pl.pallas_call(
        functools.partial(_out_proj_kernel, s_tiles=seq // tm),
        grid=(t // tm,),
        in_specs=[
            pl.BlockSpec((tm, d), lambda i: (i, 0)),
            pl.BlockSpec((tm, aw), lambda i: (i, 0)),
            rec_blk, rec_blk, rec_blk,
            pl.BlockSpec((1, aw), lambda i: (0, 0)),
            pl.BlockSpec((1, rw), lambda i: (0, 0)),
            pl.BlockSpec((aw + rw, d), lambda i: (0, 0), pipeline_mode=pl.Buffered(1)),
            pl.BlockSpec((1, d), lambda i: (0, 0)),
        ],
        out_specs=pl.BlockSpec((tm, d), lambda i: (i, 0)),
        out_shape=jax.ShapeDtypeStruct((t, d), F32),
        scratch_shapes=[pltpu.VMEM((1, rw), F32)],
        compiler_params=pltpu.CompilerParams(
            dimension_semantics=("arbitrary",), vmem_limit_bytes=VMEM_LIMIT),
        name="out_proj",
    )(x2d, attn, a, b, gg, ga.reshape(1, aw), gr.reshape(1, rw), w_out.astype(BF16), pg.reshape(1, d))


def kernel(x, ffn1_pre_g, ffn1_w_gate_up, ffn1_w_down, ffn1_post_g, mix_pre_g, w_in, conv_w, conv_b, rg_w_a, rg_b_a, rg_w_x, rg_b_x, rg_lambda, attn_out_g, rec_out_g, w_out, mix_post_g, ffn2_pre_g, ffn2_w_gate_up, ffn2_w_down, ffn2_post_g):
    batch, seq, d = x.shape
    depth = ffn1_pre_g.shape[0]
    attn_width = N_ATTN_HEADS * HEAD_DIM
    rec_width = conv_w.shape[-1]
    assert w_in.shape[-1] == 3 * attn_width + 2 * rec_width
    assert seq % 512 == 0 and seq % MOBA_BLOCK == 0
    h = x.reshape(batch * seq, d)
    for l in range(depth):
        h = _ffn(h, ffn1_pre_g[l], ffn1_w_gate_up[l], ffn1_w_down[l], ffn1_post_g[l])
        qkv, a, b, gg = _in_proj(h, mix_pre_g[l], w_in[l], conv_w[l], conv_b[l], rg_w_a[l], rg_b_a[l],
                                 rg_w_x[l], rg_b_x[l], rg_lambda[l], seq, attn_width, rec_width)
        attn = _moba(qkv, batch, seq)
        h = _out_proj(h, attn, a, b, gg, attn_out_g[l], rec_out_g[l], w_out[l], mix_post_g[l], seq)
        h = _ffn(h, ffn2_pre_g[l], ffn2_w_gate_up[l], ffn2_w_down[l], ffn2_post_g[l])
    return h.reshape(batch, seq, d)
```

```python
import functools

import jax
import jax.numpy as jnp
import numpy as np
from jax import lax
from jax.experimental import pallas as pl
from jax.experimental.pallas import tpu as pltpu

F32 = jnp.float32
BF16 = jnp.bfloat16

N_ATTN_HEADS = 8
HEAD_DIM = 128
ROT_DIM = 32
ROPE_THETA = 500000.0
MOBA_BLOCK = 256
MOBA_TOPK = 3
CONV_WIDTH = 4
RGLRU_C = 8.0
FFN_HALF = 0.5
EPS = 1e-6

VMEM_LIMIT = 56 * 1024 * 1024
LANES = 128
SUBLANES = 8

NEG_BIG = -1e30
LOG2E = 1.4426950408889634


def _rms(x, g):
    ms = jnp.mean(x * x, axis=-1, keepdims=True)
    return x * lax.rsqrt(ms + EPS) * g


def _ffn_steps(j, last, x_ref, preg_ref, postg_ref, o_ref, h_scr, weights, row_blk):
    row_blocks = [slice(r, r + row_blk) for r in range(0, x_ref.shape[0], row_blk)]

    def swiglu_down(h, w):
        wg, wu, wd = w
        g = jnp.dot(h, wg, preferred_element_type=F32)
        u = jnp.dot(h, wu, preferred_element_type=F32)
        a = (g * jax.nn.sigmoid(g) * u).astype(BF16)
        return jnp.dot(a, wd, preferred_element_type=F32)

    @pl.when(j == 0)
    def _():
        w = weights(False)
        for rows in row_blocks:
            h = _rms(x_ref[rows, :], preg_ref[...]).astype(BF16)
            h_scr[rows, :] = h
            o_ref[rows, :] = swiglu_down(h, w)

    @pl.when(jnp.logical_and(j > 0, j < last))
    def _():
        o_ref[...] += swiglu_down(h_scr[...], weights(False))

    @pl.when(j == last)
    def _():
        w = weights(True)
        for rows in row_blocks:
            acc = o_ref[rows, :] + swiglu_down(h_scr[rows, :], w)
            o_ref[rows, :] = x_ref[rows, :] + FFN_HALF * _rms(acc, postg_ref[...])


def _ffn_head_kernel(x_ref, preg_ref, wg0_ref, wg1_ref, wu0_ref, wu1_ref, wd0_ref, wd1_ref, postg_ref,
                     o_ref, cg_ref, cu_ref, cd_ref, h_scr, *, n_last, row_blk):
    j = pl.program_id(0)

    def weights(is_last):
        n = n_last if is_last else 2
        pick = lambda refs, axis: jnp.concatenate(
            [r[...].astype(BF16) if k < n else jnp.zeros(r.shape, BF16) for k, r in enumerate(refs)],
            axis=axis)
        w = pick([wg0_ref, wg1_ref], 1), pick([wu0_ref, wu1_ref], 1), pick([wd0_ref, wd1_ref], 0)
        cg_ref[...], cu_ref[...], cd_ref[...] = w
        return w

    _ffn_steps(j, pl.num_programs(0) - 1, x_ref, preg_ref, postg_ref, o_ref, h_scr, weights, row_blk)


def _ffn_tail_kernel(x_ref, preg_ref, wg_ref, wu_ref, wd_ref, postg_ref, head_out_ref, o_ref, h_scr,
                     *, row_blk):
    del head_out_ref
    weights = lambda is_last: (wg_ref[...], wu_ref[...], wd_ref[...])
    _ffn_steps(pl.program_id(1), pl.num_programs(1) - 1, x_ref, preg_ref, postg_ref, o_ref, h_scr,
               weights, row_blk)


def _ffn(x2d, pre_g, w_gate_up, w_down, post_g, *, tm=1024, tf=512, row_blk=256):
    t, d = x2d.shape
    f = w_down.shape[0]
    assert f % LANES == 0 and tf % (2 * LANES) == 0
    nblk = f // LANES
    fp = -(-f // tf) * tf
    n_head = fp // (2 * LANES)
    n_last = nblk - 2 * (n_head - 1)
    assert 1 <= n_last <= 2
    pre = pre_g.reshape(1, d)
    post = post_g.reshape(1, d)
    vec = pl.BlockSpec((1, d), lambda *idx: (0, 0))
    col = lambda k, base: pl.BlockSpec(
        (d, LANES), lambda j: (0, base + jnp.minimum(2 * j + k, nblk - 1)))
    row = lambda k: pl.BlockSpec((LANES, d), lambda j: (jnp.minimum(2 * j + k, nblk - 1), 0))
    once = pl.Buffered(1)
    head_out, cg, cu, cd = pl.pallas_call(
        functools.partial(_ffn_head_kernel, n_last=n_last, row_blk=row_blk),
        grid=(n_head,),
        in_specs=[
            pl.BlockSpec((tm, d), lambda j: (0, 0), pipeline_mode=once),
            vec,
            col(0, 0), col(1, 0), col(0, nblk), col(1, nblk),
            row(0), row(1),
            vec,
        ],
        out_specs=[
            pl.BlockSpec((tm, d), lambda j: (0, 0)),
            pl.BlockSpec((d, 2 * LANES), lambda j: (0, j)),
            pl.BlockSpec((d, 2 * LANES), lambda j: (0, j)),
            pl.BlockSpec((2 * LANES, d), lambda j: (j, 0)),
        ],
        out_shape=[
            jax.ShapeDtypeStruct((t, d), F32),
            jax.ShapeDtypeStruct((d, fp), BF16),
            jax.ShapeDtypeStruct((d, fp), BF16),
            jax.ShapeDtypeStruct((fp, d), BF16),
        ],
        scratch_shapes=[pltpu.VMEM((tm, d), BF16)],
        compiler_params=pltpu.CompilerParams(
            dimension_semantics=("arbitrary",), vmem_limit_bytes=VMEM_LIMIT),
        name="ffn_head",
    )(x2d, pre, w_gate_up, w_gate_up, w_gate_up, w_gate_up, w_down, w_down, post)
    if t == tm:
        return head_out
    return pl.pallas_call(
        functools.partial(_ffn_tail_kernel, row_blk=row_blk),
        grid=(t // tm - 1, fp // tf),
        in_specs=[
            pl.BlockSpec((tm, d), lambda i, j: (i + 1, 0)),
            vec,
            pl.BlockSpec((d, tf), lambda i, j: (0, j)),
            pl.BlockSpec((d, tf), lambda i, j: (0, j)),
            pl.BlockSpec((tf, d), lambda i, j: (j, 0)),
            vec,
            pl.BlockSpec(memory_space=pl.ANY),
        ],
        out_specs=pl.BlockSpec((tm, d), lambda i, j: (i + 1, 0)),
        out_shape=jax.ShapeDtypeStruct((t, d), F32),
        input_output_aliases={6: 0},
        scratch_shapes=[pltpu.VMEM((tm, d), BF16)],
        compiler_params=pltpu.CompilerParams(
            dimension_semantics=("parallel", "arbitrary"), vmem_limit_bytes=VMEM_LIMIT),
        name="ffn_tail",
    )(x2d, pre, cg, cu, cd, post, head_out)


def _in_proj_kernel(x_ref, g_ref, w_ref, cos_ref, sin_ref, cw_ref, cb_ref, wg_ref, ba_ref, bx_ref,
                    lam_ref, qkv_ref, a_ref, b_ref, gg_ref, xbuf,
                    *, tn, attn_width, rec_width, gate_tile, s_tiles):
    i = pl.program_id(0)
    tm = x_ref.shape[0]
    pad = SUBLANES

    @pl.when(i % s_tiles == 0)
    def _():
        xbuf[0:pad, :] = jnp.zeros((pad, rec_width), F32)

    h = _rms(x_ref[...], g_ref[...]).astype(BF16)
    c = cos_ref[...]
    s = sin_ref[...]
    lane = lax.broadcasted_iota(jnp.int32, c.shape, 1)
    first_half = lane < ROT_DIM // 2

    def proj(col0):
        return jnp.dot(h, w_ref[:, col0:col0 + tn], preferred_element_type=F32)

    def emit_qk(col0):
        y = proj(col0)
        for hh in range(tn // HEAD_DIM):
            yh = y[:, hh * HEAD_DIM:(hh + 1) * HEAD_DIM]
            partner = jnp.where(first_half,
                                pltpu.roll(yh, HEAD_DIM - ROT_DIM // 2, 1),
                                pltpu.roll(yh, ROT_DIM // 2, 1))
            col = col0 + hh * HEAD_DIM
            qkv_ref[:, col:col + HEAD_DIM] = (yh * c + partner * s).astype(BF16)

    def emit_v(col0):
        qkv_ref[:, col0:col0 + tn] = proj(col0).astype(BF16)

    def emit_gate_gelu(col0):
        gg_ref[:, col0:col0 + tn] = jax.nn.gelu(proj(3 * attn_width + rec_width + col0))

    w = cw_ref[...]

    def emit_recurrence_inputs(ct):
        cols = slice(ct * gate_tile, (ct + 1) * gate_tile)
        xc = cb_ref[:, cols] + w[CONV_WIDTH - 1:CONV_WIDTH, cols] * xbuf[pad:pad + tm, cols]
        for j in range(1, CONV_WIDTH):
            xc = xc + w[CONV_WIDTH - 1 - j:CONV_WIDTH - j, cols] * xbuf[pad - j:pad - j + tm, cols]
        gates = jnp.dot(xc.astype(BF16), wg_ref[ct], preferred_element_type=F32)
        r = jax.nn.sigmoid(gates[:, :gate_tile] + ba_ref[:, cols])
        gi = jax.nn.sigmoid(gates[:, gate_tile:] + bx_ref[:, cols])
        z = -lam_ref[:, cols]
        softplus = jnp.maximum(z, 0.0) + jnp.log1p(jnp.exp(-jnp.abs(z)))
        log_a = (-RGLRU_C * r) * softplus
        a = jnp.exp(log_a)
        a_ref[:, cols] = a
        b_ref[:, cols] = jnp.sqrt(-jnp.tanh(log_a) * (a * a + 1.0)) * (gi * xc)

    for col0 in range(0, rec_width, tn):
        xbuf[pad:pad + tm, col0:col0 + tn] = proj(3 * attn_width + col0)
    matmul_work = ([functools.partial(emit_qk, c0) for c0 in range(0, 2 * attn_width, tn)]
                   + [functools.partial(emit_v, c0) for c0 in range(2 * attn_width, 3 * attn_width, tn)]
                   + [functools.partial(emit_gate_gelu, c0) for c0 in range(0, rec_width, tn)])
    n_gate_tiles = rec_width // gate_tile
    per = -(-len(matmul_work) // n_gate_tiles)
    for ct in range(n_gate_tiles):
        emit_recurrence_inputs(ct)
        for work in matmul_work[ct * per:(ct + 1) * per]:
            work()
    for work in matmul_work[n_gate_tiles * per:]:
        work()
    xbuf[0:pad, :] = xbuf[tm:tm + pad, :]


def _rope_tables(seq):
    inv_freq = ROPE_THETA ** (-np.arange(0, ROT_DIM, 2, dtype=np.float64) / ROT_DIM)
    ang = np.arange(seq, dtype=np.float64)[:, None] * inv_freq[None, :]
    cos, sin = np.cos(ang), np.sin(ang)
    ones = np.ones((seq, HEAD_DIM - ROT_DIM))
    zeros = np.zeros((seq, HEAD_DIM - ROT_DIM))
    cos_t = np.concatenate([cos, cos, ones], axis=1)
    sin_t = np.concatenate([-sin, sin, zeros], axis=1)
    return jnp.asarray(cos_t, F32), jnp.asarray(sin_t, F32)


def _block_diag_gates(w_a, w_x, cw):
    nblk, bd, _ = w_a.shape
    per = cw // bd
    nt = nblk // per
    eye = jnp.eye(per, dtype=w_a.dtype)

    def bdiag(w):
        w = w.reshape(nt, per, bd, bd)
        full = jnp.einsum("tpij,pq->tpiqj", w, eye)
        return full.reshape(nt, cw, cw)

    return jnp.concatenate([bdiag(w_a), bdiag(w_x)], axis=2).astype(BF16)


def _in_proj(x2d, g, w_in, conv_w, conv_b, w_a, b_a, w_x, b_x, lam, seq, attn_width, rec_width,
             *, tm=256, tn=512, gate_tile=256):
    t, d = x2d.shape
    cols = w_in.shape[1]
    s_tiles = seq // tm
    cos_t, sin_t = _rope_tables(seq)
    wg = _block_diag_gates(w_a, w_x, gate_tile)
    kern = functools.partial(_in_proj_kernel, tn=tn, attn_width=attn_width, rec_width=rec_width,
                             gate_tile=gate_tile, s_tiles=s_tiles)
    whole = lambda shape: pl.BlockSpec(shape, lambda i: (0,) * len(shape))
    rec_row = lambda v: v.reshape(1, rec_width)
    rec_out = pl.BlockSpec((tm, rec_width), lambda i: (i, 0))
    rec_shape = jax.ShapeDtypeStruct((t, rec_width), F32)
    return pl.pallas_call(
        kern,
        grid=(t // tm,),
        in_specs=[
            pl.BlockSpec((tm, d), lambda i: (i, 0)),
            whole((1, d)),
            pl.BlockSpec((d, cols), lambda i: (0, 0), pipeline_mode=pl.Buffered(1)),
            pl.BlockSpec((tm, HEAD_DIM), lambda i: (i % s_tiles, 0)),
            pl.BlockSpec((tm, HEAD_DIM), lambda i: (i % s_tiles, 0)),
            whole((CONV_WIDTH, rec_width)),
            whole((1, rec_width)),
            whole(wg.shape),
            whole((1, rec_width)), whole((1, rec_width)), whole((1, rec_width)),
        ],
        out_specs=[pl.BlockSpec((tm, 3 * attn_width), lambda i: (i, 0)), rec_out, rec_out, rec_out],
        out_shape=[jax.ShapeDtypeStruct((t, 3 * attn_width), BF16), rec_shape, rec_shape, rec_shape],
        scratch_shapes=[pltpu.VMEM((tm + SUBLANES, rec_width), F32)],
        compiler_params=pltpu.CompilerParams(
            dimension_semantics=("arbitrary",), vmem_limit_bytes=VMEM_LIMIT),
        name="in_proj",
    )(x2d, g.reshape(1, d), w_in.astype(BF16), cos_t, sin_t, conv_w, rec_row(conv_b), wg,
      rec_row(b_a), rec_row(b_x), rec_row(lam))


def _moba_kernel(q_ref, k_ref, v_ref, o_ref, vt_scr, kmean_scr, s_scr, p_scr, *, nb):
    blk = MOBA_BLOCK
    scale_l2 = (HEAD_DIM ** -0.5) * LOG2E
    tb = (((1,), (1,)), ((), ()))

    ones_row = lax.broadcasted_iota(jnp.int32, (vt_scr.shape[0] - HEAD_DIM, vt_scr.shape[1]), 0) == 0
    vt_scr[HEAD_DIM:, :] = jnp.where(ones_row, 1.0, 0.0).astype(BF16)
    for n in range(nb):
        rows = slice(n * blk, (n + 1) * blk)
        vt_scr[:HEAD_DIM, rows] = v_ref[rows, :].astype(F32).T.astype(BF16)
        kmean_scr[n:n + 1, :] = jnp.mean(k_ref[rows, :].astype(F32), axis=0, keepdims=True)

    km = kmean_scr[...]
    km_hi = km.astype(BF16)
    km_lo = (km - km_hi.astype(F32)).astype(BF16)
    blk_id = lax.broadcasted_iota(jnp.int32, (nb, blk), 0)
    key_pos = lax.broadcasted_iota(jnp.int32, (blk, blk), 0)
    qry_pos = lax.broadcasted_iota(jnp.int32, (blk, blk), 1)
    causal = key_pos <= qry_pos

    def scores(qi):
        slot = qi % 2
        q = q_ref[qi * blk:(qi + 1) * blk, :]

        if qi <= MOBA_TOPK:
            bias = None
        else:
            gate = (lax.dot_general(km_hi, q, tb, preferred_element_type=F32)
                    + lax.dot_general(km_lo, q, tb, preferred_element_type=F32))
            eligible = blk_id < qi
            rem = jnp.where(eligible, gate, -jnp.inf)
            sel = jnp.zeros((nb, blk), jnp.bool_)
            for _t in range(MOBA_TOPK):
                mx = jnp.max(rem, axis=0, keepdims=True)
                first = jnp.min(jnp.where(rem == mx, blk_id, nb), axis=0, keepdims=True)
                pick = blk_id == first
                sel = jnp.logical_or(sel, pick)
                rem = jnp.where(pick, -jnp.inf, rem)
            bias = jnp.where(jnp.logical_and(sel, eligible), 0.0, NEG_BIG)

        m = None
        for n in range(qi + 1):
            rows = slice(n * blk, (n + 1) * blk)
            raw = lax.dot_general(k_ref[rows, :], q, tb, preferred_element_type=F32)
            if n == qi:
                raw = jnp.where(causal, raw, NEG_BIG)
            s_scr[slot, rows, :] = raw
            cm = jnp.max(raw, axis=0, keepdims=True) * scale_l2
            if bias is not None and n < qi:
                cm = cm + bias[n:n + 1, :]
            m = cm if m is None else jnp.maximum(m, cm)
        return bias, m

    def probabilities(qi, bias, m):
        slot = qi % 2
        for n in range(qi + 1):
            rows = slice(n * blk, (n + 1) * blk)
            shift = -m
            if bias is not None and n < qi:
                shift = bias[n:n + 1, :] - m
            x = s_scr[slot, rows, :] * scale_l2 + shift
            p_scr[slot, rows, :] = jnp.exp2(x.astype(BF16))

    def weighted_values(qi):
        slot = qi % 2
        nk = (qi + 1) * blk
        acc = jnp.dot(vt_scr[:, 0:nk], p_scr[slot, 0:nk, :], preferred_element_type=F32)
        l = acc[HEAD_DIM:HEAD_DIM + 1, :]
        o_ref[qi * blk:(qi + 1) * blk, :] = (acc[:HEAD_DIM, :] / l).T

    state = scores(0)
    for qi in range(nb):
        probabilities(qi, *state)
        state = scores(qi + 1) if qi + 1 < nb else None
        weighted_values(qi)


def _moba(qkv, batch, seq):
    t = qkv.shape[0]
    nb = seq // MOBA_BLOCK
    nh = N_ATTN_HEADS
    return pl.pallas_call(
        functools.partial(_moba_kernel, nb=nb),
        grid=(batch, nh),
        in_specs=[
            pl.BlockSpec((seq, HEAD_DIM), lambda b, h: (b, h)),
            pl.BlockSpec((seq, HEAD_DIM), lambda b, h: (b, nh + h)),
            pl.BlockSpec((seq, HEAD_DIM), lambda b, h: (b, 2 * nh + h)),
        ],
        out_specs=pl.BlockSpec((seq, HEAD_DIM), lambda b, h: (b, h)),
        out_shape=jax.ShapeDtypeStruct((t, nh * HEAD_DIM), F32),
        scratch_shapes=[
            pltpu.VMEM((HEAD_DIM + 2 * SUBLANES, seq), BF16),
            pltpu.VMEM((nb, HEAD_DIM), F32),
            pltpu.VMEM((2, seq, MOBA_BLOCK), F32),
            pltpu.VMEM((2, seq, MOBA_BLOCK), BF16),
        ],
        compiler_params=pltpu.CompilerParams(
            dimension_semantics=("parallel", "parallel"), vmem_limit_bytes=VMEM_LIMIT),
        name="moba",
    )(qkv, qkv, qkv)


def _out_proj_kernel(x_ref, attn_ref, a_ref, b_ref, gg_ref, ga_ref, gr_ref, w_ref, pg_ref, o_ref, hcar,
                     *, s_tiles):
    i = pl.program_id(0)
    tm = x_ref.shape[0]

    @pl.when(i % s_tiles == 0)
    def _():
        hcar[...] = jnp.zeros_like(hcar)

    a = a_ref[...]
    b = b_ref[...]
    row8 = lax.broadcasted_iota(jnp.int32, a.shape, 0) & (SUBLANES - 1)
    for s in (1, 2, 4):
        ar = pltpu.roll(a, s, 0)
        br = pltpu.roll(b, s, 0)
        keep = row8 >= s
        b = jnp.where(keep, a * br + b, b)
        a = jnp.where(keep, a * ar, a)
    carry = hcar[...]
    groups = []
    for r0 in range(0, tm, SUBLANES):
        hg = a[r0:r0 + SUBLANES, :] * carry + b[r0:r0 + SUBLANES, :]
        groups.append(hg)
        carry = hg[SUBLANES - 1:SUBLANES, :]
    hcar[...] = carry
    rec = jnp.concatenate(groups, axis=0) * gg_ref[...]

    at = _rms(attn_ref[...], ga_ref[...]).astype(BF16)
    rc = _rms(rec, gr_ref[...]).astype(BF16)
    aw = at.shape[1]
    y = (jnp.dot(at, w_ref[:aw, :], preferred_element_type=F32)
         + jnp.dot(rc, w_ref[aw:, :], preferred_element_type=F32))
    o_ref[...] = x_ref[...] + _rms(y, pg_ref[...])


def _out_proj(x2d, attn, a, b, gg, ga, gr, w_out, pg, seq, *, tm=512):
    t, d = x2d.shape
    aw = attn.shape[1]
    rw = a.shape[1]
    rec_blk = pl.BlockSpec((tm, rw), lambda i: (i, 0))
    return pl.pallas_call(
        functools.partial(_out_proj_kernel, s_tiles=seq // tm),
        grid=(t // tm,),
        in_specs=[
            pl.BlockSpec((tm, d), lambda i: (i, 0)),
            pl.BlockSpec((tm, aw), lambda i: (i, 0)),
            rec_blk, rec_blk, rec_blk,
            pl.BlockSpec((1, aw), lambda i: (0, 0)),
            pl.BlockSpec((1, rw), lambda i: (0, 0)),
            pl.BlockSpec((aw + rw, d), lambda i: (0, 0), pipeline_mode=pl.Buffered(1)),
            pl.BlockSpec((1, d), lambda i: (0, 0)),
        ],
        out_specs=pl.BlockSpec((tm, d), lambda i: (i, 0)),
        out_shape=jax.ShapeDtypeStruct((t, d), F32),
        scratch_shapes=[pltpu.VMEM((1, rw), F32)],
        compiler_params=pltpu.CompilerParams(
            dimension_semantics=("arbitrary",), vmem_limit_bytes=VMEM_LIMIT),
        name="out_proj",
    )(x2d, attn, a, b, gg, ga.reshape(1, aw), gr.reshape(1, rw), w_out.astype(BF16), pg.reshape(1, d))


def kernel(x, ffn1_pre_g, ffn1_w_gate_up, ffn1_w_down, ffn1_post_g, mix_pre_g, w_in, conv_w, conv_b, rg_w_a, rg_b_a, rg_w_x, rg_b_x, rg_lambda, attn_out_g, rec_out_g, w_out, mix_post_g, ffn2_pre_g, ffn2_w_gate_up, ffn2_w_down, ffn2_post_g):
    batch, seq, d = x.shape
    depth = ffn1_pre_g.shape[0]
    attn_width = N_ATTN_HEADS * HEAD_DIM
    rec_width = conv_w.shape[-1]
    assert w_in.shape[-1] == 3 * attn_width + 2 * rec_width
    assert seq % 512 == 0 and seq % MOBA_BLOCK == 0
    h = x.reshape(batch * seq, d)
    for l in range(depth):
        h = _ffn(h, ffn1_pre_g[l], ffn1_w_gate_up[l], ffn1_w_down[l], ffn1_post_g[l])
        qkv, a, b, gg = _in_proj(h, mix_pre_g[l], w_in[l], conv_w[l], conv_b[l], rg_w_a[l], rg_b_a[l],
                                 rg_w_x[l], rg_b_x[l], rg_lambda[l], seq, attn_width, rec_width)
        attn = _moba(qkv, batch, seq)
        h = _out_proj(h, attn, a, b, gg, attn_out_g[l], rec_out_g[l], w_out[l], mix_post_g[l], seq)
        h = _ffn(h, ffn2_pre_g[l], ffn2_w_gate_up[l], ffn2_w_down[l], ffn2_post_g[l])
    return h.reshape(batch, seq, d)
```

```python
import functools

import jax
import jax.numpy as jnp
import numpy as np
from jax import lax
from jax.experimental import pallas as pl
from jax.experimental.pallas import tpu as pltpu

F32 = jnp.float32
BF16 = jnp.bfloat16

N_ATTN_HEADS = 8
HEAD_DIM = 128
ROT_DIM = 32
ROPE_THETA = 500000.0
MOBA_BLOCK = 256
MOBA_TOPK = 3
CONV_WIDTH = 4
RGLRU_C = 8.0
FFN_HALF = 0.5
EPS = 1e-6

VMEM_LIMIT = 56 * 1024 * 1024
LANES = 128
SUBLANES = 8

NEG_BIG = -1e30
PACE_ROWS = 128
LOG2E = 1.4426950408889634


def _rms(x, g):
    ms = jnp.mean(x * x, axis=-1, keepdims=True)
    return x * lax.rsqrt(ms + EPS) * g


def _ffn_steps(j, last, x_ref, preg_ref, postg_ref, o_ref, h_scr, weights, row_blk):
    row_blocks = [slice(r, r + row_blk) for r in range(0, x_ref.shape[0], row_blk)]

    def swiglu_down(h, w):
        wg, wu, wd = w
        g = jnp.dot(h, wg, preferred_element_type=F32)
        u = jnp.dot(h, wu, preferred_element_type=F32)
        a = (g * jax.nn.sigmoid(g) * u).astype(BF16)
        return jnp.dot(a, wd, preferred_element_type=F32)

    @pl.when(j == 0)
    def _():
        w = weights(False)
        for rows in row_blocks:
            h = _rms(x_ref[rows, :], preg_ref[...]).astype(BF16)
            h_scr[rows, :] = h
            o_ref[rows, :] = swiglu_down(h, w)

    @pl.when(jnp.logical_and(j > 0, j < last))
    def _():
        o_ref[...] += swiglu_down(h_scr[...], weights(False))

    @pl.when(j == last)
    def _():
        w = weights(True)
        for rows in row_blocks:
            acc = o_ref[rows, :] + swiglu_down(h_scr[rows, :], w)
            o_ref[rows, :] = x_ref[rows, :] + FFN_HALF * _rms(acc, postg_ref[...])


def _ffn_head_kernel(x_ref, preg_ref, wg0_ref, wg1_ref, wu0_ref, wu1_ref, wd0_ref, wd1_ref, postg_ref,
                     o_ref, cg_ref, cu_ref, cd_ref, h_scr, *, n_last, row_blk):
    j = pl.program_id(0)

    def weights(is_last):
        n = n_last if is_last else 2
        pick = lambda refs, axis: jnp.concatenate(
            [r[...].astype(BF16) if k < n else jnp.zeros(r.shape, BF16) for k, r in enumerate(refs)],
            axis=axis)
        w = pick([wg0_ref, wg1_ref], 1), pick([wu0_ref, wu1_ref], 1), pick([wd0_ref, wd1_ref], 0)
        cg_ref[...], cu_ref[...], cd_ref[...] = w
        return w

    _ffn_steps(j, pl.num_programs(0) - 1, x_ref, preg_ref, postg_ref, o_ref, h_scr, weights, row_blk)


def _ffn_tail_kernel(x_ref, preg_ref, wg_ref, wu_ref, wd_ref, postg_ref, head_out_ref, o_ref, h_scr,
                     *, row_blk):
    del head_out_ref
    weights = lambda is_last: (wg_ref[...], wu_ref[...], wd_ref[...])
    _ffn_steps(pl.program_id(1), pl.num_programs(1) - 1, x_ref, preg_ref, postg_ref, o_ref, h_scr,
               weights, row_blk)


def _ffn(x2d, pre_g, w_gate_up, w_down, post_g, *, tm=1024, tf=512, row_blk=256):
    t, d = x2d.shape
    f = w_down.shape[0]
    assert f % LANES == 0 and tf % (2 * LANES) == 0
    nblk = f // LANES
    fp = -(-f // tf) * tf
    n_head = fp // (2 * LANES)
    n_last = nblk - 2 * (n_head - 1)
    assert 1 <= n_last <= 2
    pre = pre_g.reshape(1, d)
    post = post_g.reshape(1, d)
    vec = pl.BlockSpec((1, d), lambda *idx: (0, 0))
    col = lambda k, base: pl.BlockSpec(
        (d, LANES), lambda j: (0, base + jnp.minimum(2 * j + k, nblk - 1)))
    row = lambda k: pl.BlockSpec((LANES, d), lambda j: (jnp.minimum(2 * j + k, nblk - 1), 0))
    once = pl.Buffered(1)
    head_out, cg, cu, cd = pl.pallas_call(
        functools.partial(_ffn_head_kernel, n_last=n_last, row_blk=row_blk),
        grid=(n_head,),
        in_specs=[
            pl.BlockSpec((tm, d), lambda j: (0, 0), pipeline_mode=once),
            vec,
            col(0, 0), col(1, 0), col(0, nblk), col(1, nblk),
            row(0), row(1),
            vec,
        ],
        out_specs=[
            pl.BlockSpec((tm, d), lambda j: (0, 0)),
            pl.BlockSpec((d, 2 * LANES), lambda j: (0, j)),
            pl.BlockSpec((d, 2 * LANES), lambda j: (0, j)),
            pl.BlockSpec((2 * LANES, d), lambda j: (j, 0)),
        ],
        out_shape=[
            jax.ShapeDtypeStruct((t, d), F32),
            jax.ShapeDtypeStruct((d, fp), BF16),
            jax.ShapeDtypeStruct((d, fp), BF16),
            jax.ShapeDtypeStruct((fp, d), BF16),
        ],
        scratch_shapes=[pltpu.VMEM((tm, d), BF16)],
        compiler_params=pltpu.CompilerParams(
            dimension_semantics=("arbitrary",), vmem_limit_bytes=VMEM_LIMIT),
        name="ffn_head",
    )(x2d, pre, w_gate_up, w_gate_up, w_gate_up, w_gate_up, w_down, w_down, post)
    if t == tm:
        return head_out
    return pl.pallas_call(
        functools.partial(_ffn_tail_kernel, row_blk=row_blk),
        grid=(t // tm - 1, fp // tf),
        in_specs=[
            pl.BlockSpec((tm, d), lambda i, j: (i + 1, 0)),
            vec,
            pl.BlockSpec((d, tf), lambda i, j: (0, j)),
            pl.BlockSpec((d, tf), lambda i, j: (0, j)),
            pl.BlockSpec((tf, d), lambda i, j: (j, 0)),
            vec,
            pl.BlockSpec(memory_space=pl.ANY),
        ],
        out_specs=pl.BlockSpec((tm, d), lambda i, j: (i + 1, 0)),
        out_shape=jax.ShapeDtypeStruct((t, d), F32),
        input_output_aliases={6: 0},
        scratch_shapes=[pltpu.VMEM((tm, d), BF16)],
        compiler_params=pltpu.CompilerParams(
            dimension_semantics=("parallel", "arbitrary"), vmem_limit_bytes=VMEM_LIMIT),
        name="ffn_tail",
    )(x2d, pre, cg, cu, cd, post, head_out)


def _in_proj_kernel(x_ref, g_ref, w_ref, cos_ref, sin_ref, cw_ref, cb_ref, wg_ref, ba_ref, bx_ref,
                    lam_ref, qkv_ref, a_ref, b_ref, gg_ref, xbuf,
                    *, tn, attn_width, rec_width, gate_tile, s_tiles):
    i = pl.program_id(0)
    tm = x_ref.shape[0]
    pad = SUBLANES

    @pl.when(i % s_tiles == 0)
    def _():
        xbuf[0:pad, :] = jnp.zeros((pad, rec_width), F32)

    h = _rms(x_ref[...], g_ref[...]).astype(BF16)
    c = cos_ref[...]
    s = sin_ref[...]
    lane = lax.broadcasted_iota(jnp.int32, c.shape, 1)
    first_half = lane < ROT_DIM // 2

    def proj(col0):
        return jnp.dot(h, w_ref[:, col0:col0 + tn], preferred_element_type=F32)

    def emit_qk(col0):
        y = proj(col0)
        for hh in range(tn // HEAD_DIM):
            yh = y[:, hh * HEAD_DIM:(hh + 1) * HEAD_DIM]
            partner = jnp.where(first_half,
                                pltpu.roll(yh, HEAD_DIM - ROT_DIM // 2, 1),
                                pltpu.roll(yh, ROT_DIM // 2, 1))
            col = col0 + hh * HEAD_DIM
            qkv_ref[:, col:col + HEAD_DIM] = (yh * c + partner * s).astype(BF16)

    def emit_v(col0):
        qkv_ref[:, col0:col0 + tn] = proj(col0).astype(BF16)

    def emit_gate_gelu(col0):
        gg_ref[:, col0:col0 + tn] = jax.nn.gelu(proj(3 * attn_width + rec_width + col0))

    w = cw_ref[...]

    def emit_recurrence_inputs(ct):
        cols = slice(ct * gate_tile, (ct + 1) * gate_tile)
        xc = cb_ref[:, cols] + w[CONV_WIDTH - 1:CONV_WIDTH, cols] * xbuf[pad:pad + tm, cols]
        for j in range(1, CONV_WIDTH):
            xc = xc + w[CONV_WIDTH - 1 - j:CONV_WIDTH - j, cols] * xbuf[pad - j:pad - j + tm, cols]
        gates = jnp.dot(xc.astype(BF16), wg_ref[ct], preferred_element_type=F32)
        r = jax.nn.sigmoid(gates[:, :gate_tile] + ba_ref[:, cols])
        gi = jax.nn.sigmoid(gates[:, gate_tile:] + bx_ref[:, cols])
        z = -lam_ref[:, cols]
        softplus = jnp.maximum(z, 0.0) + jnp.log1p(jnp.exp(-jnp.abs(z)))
        log_a = (-RGLRU_C * r) * softplus
        a = jnp.exp(log_a)
        a_ref[:, cols] = a
        b_ref[:, cols] = jnp.sqrt(-jnp.tanh(log_a) * (a * a + 1.0)) * (gi * xc)

    for col0 in range(0, rec_width, tn):
        xbuf[pad:pad + tm, col0:col0 + tn] = proj(3 * attn_width + col0)
    matmul_work = ([functools.partial(emit_qk, c0) for c0 in range(0, 2 * attn_width, tn)]
                   + [functools.partial(emit_v, c0) for c0 in range(2 * attn_width, 3 * attn_width, tn)]
                   + [functools.partial(emit_gate_gelu, c0) for c0 in range(0, rec_width, tn)])
    n_gate_tiles = rec_width // gate_tile
    per = -(-len(matmul_work) // n_gate_tiles)
    for ct in range(n_gate_tiles):
        emit_recurrence_inputs(ct)
        for work in matmul_work[ct * per:(ct + 1) * per]:
            work()
    for work in matmul_work[n_gate_tiles * per:]:
        work()
    xbuf[0:pad, :] = xbuf[tm:tm + pad, :]


def _rope_tables(seq):
    inv_freq = ROPE_THETA ** (-np.arange(0, ROT_DIM, 2, dtype=np.float64) / ROT_DIM)
    ang = np.arange(seq, dtype=np.float64)[:, None] * inv_freq[None, :]
    cos, sin = np.cos(ang), np.sin(ang)
    ones = np.ones((seq, HEAD_DIM - ROT_DIM))
    zeros = np.zeros((seq, HEAD_DIM - ROT_DIM))
    cos_t = np.concatenate([cos, cos, ones], axis=1)
    sin_t = np.concatenate([-sin, sin, zeros], axis=1)
    return jnp.asarray(cos_t, F32), jnp.asarray(sin_t, F32)


def _block_diag_gates(w_a, w_x, cw):
    nblk, bd, _ = w_a.shape
    per = cw // bd
    nt = nblk // per
    eye = jnp.eye(per, dtype=w_a.dtype)

    def bdiag(w):
        w = w.reshape(nt, per, bd, bd)
        full = jnp.einsum("tpij,pq->tpiqj", w, eye)
        return full.reshape(nt, cw, cw)

    return jnp.concatenate([bdiag(w_a), bdiag(w_x)], axis=2).astype(BF16)


def _in_proj(x2d, g, w_in, conv_w, conv_b, w_a, b_a, w_x, b_x, lam, seq, attn_width, rec_width,
             *, tm=256, tn=512, gate_tile=256):
    t, d = x2d.shape
    cols = w_in.shape[1]
    s_tiles = seq // tm
    cos_t, sin_t = _rope_tables(seq)
    wg = _block_diag_gates(w_a, w_x, gate_tile)
    kern = functools.partial(_in_proj_kernel, tn=tn, attn_width=attn_width, rec_width=rec_width,
                             gate_tile=gate_tile, s_tiles=s_tiles)
    whole = lambda shape: pl.BlockSpec(shape, lambda i: (0,) * len(shape))
    rec_row = lambda v: v.reshape(1, rec_width)
    rec_out = pl.BlockSpec((tm, rec_width), lambda i: (i, 0))
    rec_shape = jax.ShapeDtypeStruct((t, rec_width), F32)
    return pl.pallas_call(
        kern,
        grid=(t // tm,),
        in_specs=[
            pl.BlockSpec((tm, d), lambda i: (i, 0)),
            whole((1, d)),
            pl.BlockSpec((d, cols), lambda i: (0, 0), pipeline_mode=pl.Buffered(1)),
            pl.BlockSpec((tm, HEAD_DIM), lambda i: (i % s_tiles, 0)),
            pl.BlockSpec((tm, HEAD_DIM), lambda i: (i % s_tiles, 0)),
            whole((CONV_WIDTH, rec_width)),
            whole((1, rec_width)),
            whole(wg.shape),
            whole((1, rec_width)), whole((1, rec_width)), whole((1, rec_width)),
        ],
        out_specs=[pl.BlockSpec((tm, 3 * attn_width), lambda i: (i, 0)), rec_out, rec_out, rec_out],
        out_shape=[jax.ShapeDtypeStruct((t, 3 * attn_width), BF16), rec_shape, rec_shape, rec_shape],
        scratch_shapes=[pltpu.VMEM((tm + SUBLANES, rec_width), F32)],
        compiler_params=pltpu.CompilerParams(
            dimension_semantics=("arbitrary",), vmem_limit_bytes=VMEM_LIMIT),
        name="in_proj",
    )(x2d, g.reshape(1, d), w_in.astype(BF16), cos_t, sin_t, conv_w, rec_row(conv_b), wg,
      rec_row(b_a), rec_row(b_x), rec_row(lam))


def _moba_kernel(q_ref, k_ref, v_ref, o_ref, vt_scr, kmean_scr, s_scr, p_scr, *, nb):
    blk = MOBA_BLOCK
    scale_l2 = (HEAD_DIM ** -0.5) * LOG2E
    tb = (((1,), (1,)), ((), ()))

    ones_row = lax.broadcasted_iota(jnp.int32, (vt_scr.shape[0] - HEAD_DIM, vt_scr.shape[1]), 0) == 0
    vt_scr[HEAD_DIM:, :] = jnp.where(ones_row, 1.0, 0.0).astype(BF16)
    for n in range(nb):
        rows = slice(n * blk, (n + 1) * blk)
        vt_scr[:HEAD_DIM, rows] = v_ref[rows, :].astype(F32).T.astype(BF16)
        kmean_scr[n:n + 1, :] = jnp.mean(k_ref[rows, :].astype(F32), axis=0, keepdims=True)

    km = kmean_scr[...]
    km_hi = km.astype(BF16)
    km_lo = (km - km_hi.astype(F32)).astype(BF16)
    blk_id = lax.broadcasted_iota(jnp.int32, (nb, blk), 0)
    key_pos = lax.broadcasted_iota(jnp.int32, (blk, blk), 0)
    qry_pos = lax.broadcasted_iota(jnp.int32, (blk, blk), 1)
    causal = key_pos <= qry_pos

    def scores(qi):
        slot = qi % 2
        q = q_ref[qi * blk:(qi + 1) * blk, :]

        if qi <= MOBA_TOPK:
            bias = None
        else:
            gate = (lax.dot_general(km_hi, q, tb, preferred_element_type=F32)
                    + lax.dot_general(km_lo, q, tb, preferred_element_type=F32))
            eligible = blk_id < qi
            rem = jnp.where(eligible, gate, -jnp.inf)
            sel = jnp.zeros((nb, blk), jnp.bool_)
            for _t in range(MOBA_TOPK):
                mx = jnp.max(rem, axis=0, keepdims=True)
                first = jnp.min(jnp.where(rem == mx, blk_id, nb), axis=0, keepdims=True)
                pick = blk_id == first
                sel = jnp.logical_or(sel, pick)
                rem = jnp.where(pick, -jnp.inf, rem)
            bias = jnp.where(jnp.logical_and(sel, eligible), 0.0, NEG_BIG)

        m = None
        for n in range(qi + 1):
            rows = slice(n * blk, (n + 1) * blk)
            raw = lax.dot_general(k_ref[rows, :], q, tb, preferred_element_type=F32)
            if n == qi:
                raw = jnp.where(causal, raw, NEG_BIG)
            s_scr[slot, rows, :] = raw
            cm = jnp.max(raw, axis=0, keepdims=True) * scale_l2
            if bias is not None and n < qi:
                cm = cm + bias[n:n + 1, :]
            m = cm if m is None else jnp.maximum(m, cm)
        return bias, m

    def probabilities(qi, bias, m):
        slot = qi % 2
        pace = None
        for n in range(qi + 1):
            shift = -m
            if bias is not None and n < qi:
                shift = bias[n:n + 1, :] - m
            for r0 in range(n * blk, (n + 1) * blk, PACE_ROWS):
                rows = slice(r0, r0 + PACE_ROWS)
                sh = shift if pace is None else shift + pace
                x = s_scr[slot, rows, :] * scale_l2 + sh
                p = jnp.exp2(x.astype(BF16))
                p_scr[slot, rows, :] = p
                last = p[PACE_ROWS - 2 * SUBLANES:, :].astype(F32)
                pace = jnp.where(jnp.min(last, axis=0, keepdims=True) < 0.0, 1.0, 0.0)

    def weighted_values(qi):
        slot = qi % 2
        nk = (qi + 1) * blk
        acc = jnp.dot(vt_scr[:, 0:nk], p_scr[slot, 0:nk, :], preferred_element_type=F32)
        l = acc[HEAD_DIM:HEAD_DIM + 1, :]
        o_ref[qi * blk:(qi + 1) * blk, :] = (acc[:HEAD_DIM, :] / l).T

    state = scores(0)
    for qi in range(nb):
        probabilities(qi, *state)
        state = scores(qi + 1) if qi + 1 < nb else None
        weighted_values(qi)


def _moba(qkv, batch, seq):
    t = qkv.shape[0]
    nb = seq // MOBA_BLOCK
    nh = N_ATTN_HEADS
    return pl.pallas_call(
        functools.partial(_moba_kernel, nb=nb),
        grid=(batch, nh),
        in_specs=[
            pl.BlockSpec((seq, HEAD_DIM), lambda b, h: (b, h)),
            pl.BlockSpec((seq, HEAD_DIM), lambda b, h: (b, nh + h)),
            pl.BlockSpec((seq, HEAD_DIM), lambda b, h: (b, 2 * nh + h)),
        ],
        out_specs=pl.BlockSpec((seq, HEAD_DIM), lambda b, h: (b, h)),
        out_shape=jax.ShapeDtypeStruct((t, nh * HEAD_DIM), F32),
        scratch_shapes=[
            pltpu.VMEM((HEAD_DIM + 2 * SUBLANES, seq), BF16),
            pltpu.VMEM((nb, HEAD_DIM), F32),
            pltpu.VMEM((2, seq, MOBA_BLOCK), F32),
            pltpu.VMEM((2, seq, MOBA_BLOCK), BF16),
        ],
        compiler_params=pltpu.CompilerParams(
            dimension_semantics=("parallel", "parallel"), vmem_limit_bytes=VMEM_LIMIT),
        name="moba",
    )(qkv, qkv, qkv)


def _out_proj_kernel(x_ref, attn_ref, a_ref, b_ref, gg_ref, ga_ref, gr_ref, w_ref, pg_ref, o_ref, hcar,
                     *, s_tiles):
    i = pl.program_id(0)
    tm = x_ref.shape[0]

    @pl.when(i % s_tiles == 0)
    def _():
        hcar[...] = jnp.zeros_like(hcar)

    a = a_ref[...]
    b = b_ref[...]
    row8 = lax.broadcasted_iota(jnp.int32, a.shape, 0) & (SUBLANES - 1)
    for s in (1, 2, 4):
        ar = pltpu.roll(a, s, 0)
        br = pltpu.roll(b, s, 0)
        keep = row8 >= s
        b = jnp.where(keep, a * br + b, b)
        a = jnp.where(keep, a * ar, a)
    carry = hcar[...]
    groups = []
    for r0 in range(0, tm, SUBLANES):
        hg = a[r0:r0 + SUBLANES, :] * carry + b[r0:r0 + SUBLANES, :]
        groups.append(hg)
        carry = hg[SUBLANES - 1:SUBLANES, :]
    hcar[...] = carry
    rec = jnp.concatenate(groups, axis=0) * gg_ref[...]

    at = _rms(attn_ref[...], ga_ref[...]).astype(BF16)
    rc = _rms(rec, gr_ref[...]).astype(BF16)
    aw = at.shape[1]
    y = (jnp.dot(at, w_ref[:aw, :], preferred_element_type=F32)
         + jnp.dot(rc, w_ref[aw:, :], preferred_element_type=F32))
    o_ref[...] = x_ref[...] + _rms(y, pg_ref[...])


def _out_proj(x2d, attn, a, b, gg, ga, gr, w_out, pg, seq, *, tm=512):
    t, d = x2d.shape
    aw = attn.shape[1]
    rw = a.shape[1]
    rec_blk = pl.BlockSpec((tm, rw), lambda i: (i, 0))
    return pl.pallas_call(
        functools.partial(_out_proj_kernel, s_tiles=seq // tm),
        grid=(t // tm,),
        in_specs=[
            pl.BlockSpec((tm, d), lambda i: (i, 0)),
            pl.BlockSpec((tm, aw), lambda i: (i, 0)),
            rec_blk, rec_blk, rec_blk,
            pl.BlockSpec((1, aw), lambda i: (0, 0)),
            pl.BlockSpec((1, rw), lambda i: (0, 0)),
            pl.BlockSpec((aw + rw, d), lambda i: (0, 0), pipeline_mode=pl.Buffered(1)),
            pl.BlockSpec((1, d), lambda i: (0, 0)),
        ],
        out_specs=pl.BlockSpec((tm, d), lambda i: (i, 0)),
        out_shape=jax.ShapeDtypeStruct((t, d), F32),
        scratch_shapes=[pltpu.VMEM((1, rw), F32)],
        compiler_params=pltpu.CompilerParams(
            dimension_semantics=("arbitrary",), vmem_limit_bytes=VMEM_LIMIT),
        name="out_proj",
    )(x2d, attn, a, b, gg, ga.reshape(1, aw), gr.reshape(1, rw), w_out.astype(BF16), pg.reshape(1, d))


def kernel(x, ffn1_pre_g, ffn1_w_gate_up, ffn1_w_down, ffn1_post_g, mix_pre_g, w_in, conv_w, conv_b, rg_w_a, rg_b_a, rg_w_x, rg_b_x, rg_lambda, attn_out_g, rec_out_g, w_out, mix_post_g, ffn2_pre_g, ffn2_w_gate_up, ffn2_w_down, ffn2_post_g):
    batch, seq, d = x.shape
    depth = ffn1_pre_g.shape[0]
    attn_width = N_ATTN_HEADS * HEAD_DIM
    rec_width = conv_w.shape[-1]
    assert w_in.shape[-1] == 3 * attn_width + 2 * rec_width
    assert seq % 512 == 0 and seq % MOBA_BLOCK == 0
    h = x.reshape(batch * seq, d)
    for l in range(depth):
        h = _ffn(h, ffn1_pre_g[l], ffn1_w_gate_up[l], ffn1_w_down[l], ffn1_post_g[l])
        qkv, a, b, gg = _in_proj(h, mix_pre_g[l], w_in[l], conv_w[l], conv_b[l], rg_w_a[l], rg_b_a[l],
                                 rg_w_x[l], rg_b_x[l], rg_lambda[l], seq, attn_width, rec_width)
        attn = _moba(qkv, batch, seq)
        h = _out_proj(h, attn, a, b, gg, attn_out_g[l], rec_out_g[l], w_out[l], mix_post_g[l], seq)
        h = _ffn(h, ffn2_pre_g[l], ffn2_w_gate_up[l], ffn2_w_down[l], ffn2_post_g[l])
    return h.reshape(batch, seq, d)
```

```python
import functools

import jax
import jax.numpy as jnp
import numpy as np
from jax import lax
from jax.experimental import pallas as pl
from jax.experimental.pallas import tpu as pltpu

F32 = jnp.float32
BF16 = jnp.bfloat16

N_ATTN_HEADS = 8
HEAD_DIM = 128
ROT_DIM = 32
ROPE_THETA = 500000.0
MOBA_BLOCK = 256
MOBA_TOPK = 3
CONV_WIDTH = 4
RGLRU_C = 8.0
FFN_HALF = 0.5
EPS = 1e-6

VMEM_LIMIT = 56 * 1024 * 1024
LANES = 128
SUBLANES = 8

NEG_BIG = -1e30
PACE_ROWS = 128
LOG2E = 1.4426950408889634


def _rms(x, g):
    ms = jnp.mean(x * x, axis=-1, keepdims=True)
    return x * lax.rsqrt(ms + EPS) * g


def _ffn_steps(j, last, x_ref, preg_ref, postg_ref, o_ref, h_scr, weights, row_blk):
    row_blocks = [slice(r, r + row_blk) for r in range(0, x_ref.shape[0], row_blk)]

    def swiglu_down(h, w):
        wg, wu, wd = w
        g = jnp.dot(h, wg, preferred_element_type=F32)
        u = jnp.dot(h, wu, preferred_element_type=F32)
        a = (g * jax.nn.sigmoid(g) * u).astype(BF16)
        return jnp.dot(a, wd, preferred_element_type=F32)

    @pl.when(j == 0)
    def _():
        w = weights(False)
        for rows in row_blocks:
            h = _rms(x_ref[rows, :], preg_ref[...]).astype(BF16)
            h_scr[rows, :] = h
            o_ref[rows, :] = swiglu_down(h, w)

    @pl.when(jnp.logical_and(j > 0, j < last))
    def _():
        o_ref[...] += swiglu_down(h_scr[...], weights(False))

    @pl.when(j == last)
    def _():
        w = weights(True)
        paces = []
        for r, rows in enumerate(row_blocks):
            h = h_scr[rows, :]
            if r >= 2:
                h = h + paces[r - 2]
            acc = o_ref[rows, :] + swiglu_down(h, w)
            out = x_ref[rows, :] + FFN_HALF * _rms(acc, postg_ref[...])
            o_ref[rows, :] = out
            tail = jnp.max(out[row_blk - SUBLANES:, :], axis=0, keepdims=True)
            paces.append(jnp.where(tail > jnp.inf, 1.0, 0.0).astype(BF16))


def _ffn_head_kernel(x_ref, preg_ref, wg0_ref, wg1_ref, wu0_ref, wu1_ref, wd0_ref, wd1_ref, postg_ref,
                     o_ref, cg_ref, cu_ref, cd_ref, h_scr, *, n_last, row_blk):
    j = pl.program_id(0)

    def weights(is_last):
        n = n_last if is_last else 2
        pick = lambda refs, axis: jnp.concatenate(
            [r[...].astype(BF16) if k < n else jnp.zeros(r.shape, BF16) for k, r in enumerate(refs)],
            axis=axis)
        w = pick([wg0_ref, wg1_ref], 1), pick([wu0_ref, wu1_ref], 1), pick([wd0_ref, wd1_ref], 0)
        cg_ref[...], cu_ref[...], cd_ref[...] = w
        return w

    _ffn_steps(j, pl.num_programs(0) - 1, x_ref, preg_ref, postg_ref, o_ref, h_scr, weights, row_blk)


def _ffn_tail_kernel(x_ref, preg_ref, wg_ref, wu_ref, wd_ref, postg_ref, head_out_ref, o_ref, h_scr,
                     *, row_blk):
    del head_out_ref
    weights = lambda is_last: (wg_ref[...], wu_ref[...], wd_ref[...])
    _ffn_steps(pl.program_id(1), pl.num_programs(1) - 1, x_ref, preg_ref, postg_ref, o_ref, h_scr,
               weights, row_blk)


def _ffn(x2d, pre_g, w_gate_up, w_down, post_g, *, tm=1024, tf=512, row_blk=256):
    t, d = x2d.shape
    f = w_down.shape[0]
    assert f % LANES == 0 and tf % (2 * LANES) == 0
    nblk = f // LANES
    fp = -(-f // tf) * tf
    n_head = fp // (2 * LANES)
    n_last = nblk - 2 * (n_head - 1)
    assert 1 <= n_last <= 2
    pre = pre_g.reshape(1, d)
    post = post_g.reshape(1, d)
    vec = pl.BlockSpec((1, d), lambda *idx: (0, 0))
    col = lambda k, base: pl.BlockSpec(
        (d, LANES), lambda j: (0, base + jnp.minimum(2 * j + k, nblk - 1)))
    row = lambda k: pl.BlockSpec((LANES, d), lambda j: (jnp.minimum(2 * j + k, nblk - 1), 0))
    once = pl.Buffered(1)
    head_out, cg, cu, cd = pl.pallas_call(
        functools.partial(_ffn_head_kernel, n_last=n_last, row_blk=row_blk),
        grid=(n_head,),
        in_specs=[
            pl.BlockSpec((tm, d), lambda j: (0, 0), pipeline_mode=once),
            vec,
            col(0, 0), col(1, 0), col(0, nblk), col(1, nblk),
            row(0), row(1),
            vec,
        ],
        out_specs=[
            pl.BlockSpec((tm, d), lambda j: (0, 0)),
            pl.BlockSpec((d, 2 * LANES), lambda j: (0, j)),
            pl.BlockSpec((d, 2 * LANES), lambda j: (0, j)),
            pl.BlockSpec((2 * LANES, d), lambda j: (j, 0)),
        ],
        out_shape=[
            jax.ShapeDtypeStruct((t, d), F32),
            jax.ShapeDtypeStruct((d, fp), BF16),
            jax.ShapeDtypeStruct((d, fp), BF16),
            jax.ShapeDtypeStruct((fp, d), BF16),
        ],
        scratch_shapes=[pltpu.VMEM((tm, d), BF16)],
        compiler_params=pltpu.CompilerParams(
            dimension_semantics=("arbitrary",), vmem_limit_bytes=VMEM_LIMIT),
        name="ffn_head",
    )(x2d, pre, w_gate_up, w_gate_up, w_gate_up, w_gate_up, w_down, w_down, post)
    if t == tm:
        return head_out
    return pl.pallas_call(
        functools.partial(_ffn_tail_kernel, row_blk=row_blk),
        grid=(t // tm - 1, fp // tf),
        in_specs=[
            pl.BlockSpec((tm, d), lambda i, j: (i + 1, 0)),
            vec,
            pl.BlockSpec((d, tf), lambda i, j: (0, j)),
            pl.BlockSpec((d, tf), lambda i, j: (0, j)),
            pl.BlockSpec((tf, d), lambda i, j: (j, 0)),
            vec,
            pl.BlockSpec(memory_space=pl.ANY),
        ],
        out_specs=pl.BlockSpec((tm, d), lambda i, j: (i + 1, 0)),
        out_shape=jax.ShapeDtypeStruct((t, d), F32),
        input_output_aliases={6: 0},
        scratch_shapes=[pltpu.VMEM((tm, d), BF16)],
        compiler_params=pltpu.CompilerParams(
            dimension_semantics=("parallel", "arbitrary"), vmem_limit_bytes=VMEM_LIMIT),
        name="ffn_tail",
    )(x2d, pre, cg, cu, cd, post, head_out)


def _in_proj_kernel(x_ref, g_ref, w_ref, cos_ref, sin_ref, cw_ref, cb_ref, wg_ref, ba_ref, bx_ref,
                    lam_ref, qkv_ref, a_ref, b_ref, gg_ref, xbuf,
                    *, tn, attn_width, rec_width, gate_tile, s_tiles):
    i = pl.program_id(0)
    tm = x_ref.shape[0]
    pad = SUBLANES

    @pl.when(i % s_tiles == 0)
    def _():
        xbuf[0:pad, :] = jnp.zeros((pad, rec_width), F32)

    h = _rms(x_ref[...], g_ref[...]).astype(BF16)
    c = cos_ref[...]
    s = sin_ref[...]
    lane = lax.broadcasted_iota(jnp.int32, c.shape, 1)
    first_half = lane < ROT_DIM // 2

    def proj(col0):
        return jnp.dot(h, w_ref[:, col0:col0 + tn], preferred_element_type=F32)

    def emit_qk(col0):
        y = proj(col0)
        for hh in range(tn // HEAD_DIM):
            yh = y[:, hh * HEAD_DIM:(hh + 1) * HEAD_DIM]
            partner = jnp.where(first_half,
                                pltpu.roll(yh, HEAD_DIM - ROT_DIM // 2, 1),
                                pltpu.roll(yh, ROT_DIM // 2, 1))
            col = col0 + hh * HEAD_DIM
            qkv_ref[:, col:col + HEAD_DIM] = (yh * c + partner * s).astype(BF16)

    def emit_v(col0):
        qkv_ref[:, col0:col0 + tn] = proj(col0).astype(BF16)

    def emit_gate_gelu(col0):
        gg_ref[:, col0:col0 + tn] = jax.nn.gelu(proj(3 * attn_width + rec_width + col0))

    w = cw_ref[...]

    def emit_recurrence_inputs(ct):
        cols = slice(ct * gate_tile, (ct + 1) * gate_tile)
        xc = cb_ref[:, cols] + w[CONV_WIDTH - 1:CONV_WIDTH, cols] * xbuf[pad:pad + tm, cols]
        for j in range(1, CONV_WIDTH):
            xc = xc + w[CONV_WIDTH - 1 - j:CONV_WIDTH - j, cols] * xbuf[pad - j:pad - j + tm, cols]
        gates = jnp.dot(xc.astype(BF16), wg_ref[ct], preferred_element_type=F32)
        r = jax.nn.sigmoid(gates[:, :gate_tile] + ba_ref[:, cols])
        gi = jax.nn.sigmoid(gates[:, gate_tile:] + bx_ref[:, cols])
        z = -lam_ref[:, cols]
        softplus = jnp.maximum(z, 0.0) + jnp.log1p(jnp.exp(-jnp.abs(z)))
        log_a = (-RGLRU_C * r) * softplus
        a = jnp.exp(log_a)
        a_ref[:, cols] = a
        b_ref[:, cols] = jnp.sqrt(-jnp.tanh(log_a) * (a * a + 1.0)) * (gi * xc)

    for col0 in range(0, rec_width, tn):
        xbuf[pad:pad + tm, col0:col0 + tn] = proj(3 * attn_width + col0)
    matmul_work = ([functools.partial(emit_qk, c0) for c0 in range(0, 2 * attn_width, tn)]
                   + [functools.partial(emit_v, c0) for c0 in range(2 * attn_width, 3 * attn_width, tn)]
                   + [functools.partial(emit_gate_gelu, c0) for c0 in range(0, rec_width, tn)])
    n_gate_tiles = rec_width // gate_tile
    per = -(-len(matmul_work) // n_gate_tiles)
    for ct in range(n_gate_tiles):
        emit_recurrence_inputs(ct)
        for work in matmul_work[ct * per:(ct + 1) * per]:
            work()
    for work in matmul_work[n_gate_tiles * per:]:
        work()
    xbuf[0:pad, :] = xbuf[tm:tm + pad, :]


def _rope_tables(seq):
    inv_freq = ROPE_THETA ** (-np.arange(0, ROT_DIM, 2, dtype=np.float64) / ROT_DIM)
    ang = np.arange(seq, dtype=np.float64)[:, None] * inv_freq[None, :]
    cos, sin = np.cos(ang), np.sin(ang)
    ones = np.ones((seq, HEAD_DIM - ROT_DIM))
    zeros = np.zeros((seq, HEAD_DIM - ROT_DIM))
    cos_t = np.concatenate([cos, cos, ones], axis=1)
    sin_t = np.concatenate([-sin, sin, zeros], axis=1)
    return jnp.asarray(cos_t, F32), jnp.asarray(sin_t, F32)


def _block_diag_gates(w_a, w_x, cw):
    nblk, bd, _ = w_a.shape
    per = cw // bd
    nt = nblk // per
    eye = jnp.eye(per, dtype=w_a.dtype)

    def bdiag(w):
        w = w.reshape(nt, per, bd, bd)
        full = jnp.einsum("tpij,pq->tpiqj", w, eye)
        return full.reshape(nt, cw, cw)

    return jnp.concatenate([bdiag(w_a), bdiag(w_x)], axis=2).astype(BF16)


def _in_proj(x2d, g, w_in, conv_w, conv_b, w_a, b_a, w_x, b_x, lam, seq, attn_width, rec_width,
             *, tm=512, tn=512, gate_tile=256):
    t, d = x2d.shape
    cols = w_in.shape[1]
    s_tiles = seq // tm
    cos_t, sin_t = _rope_tables(seq)
    wg = _block_diag_gates(w_a, w_x, gate_tile)
    kern = functools.partial(_in_proj_kernel, tn=tn, attn_width=attn_width, rec_width=rec_width,
                             gate_tile=gate_tile, s_tiles=s_tiles)
    whole = lambda shape: pl.BlockSpec(shape, lambda i: (0,) * len(shape))
    rec_row = lambda v: v.reshape(1, rec_width)
    rec_out = pl.BlockSpec((tm, rec_width), lambda i: (i, 0))
    rec_shape = jax.ShapeDtypeStruct((t, rec_width), F32)
    return pl.pallas_call(
        kern,
        grid=(t // tm,),
        in_specs=[
            pl.BlockSpec((tm, d), lambda i: (i, 0)),
            whole((1, d)),
            pl.BlockSpec((d, cols), lambda i: (0, 0), pipeline_mode=pl.Buffered(1)),
            pl.BlockSpec((tm, HEAD_DIM), lambda i: (i % s_tiles, 0)),
            pl.BlockSpec((tm, HEAD_DIM), lambda i: (i % s_tiles, 0)),
            whole((CONV_WIDTH, rec_width)),
            whole((1, rec_width)),
            whole(wg.shape),
            whole((1, rec_width)), whole((1, rec_width)), whole((1, rec_width)),
        ],
        out_specs=[pl.BlockSpec((tm, 3 * attn_width), lambda i: (i, 0)), rec_out, rec_out, rec_out],
        out_shape=[jax.ShapeDtypeStruct((t, 3 * attn_width), BF16), rec_shape, rec_shape, rec_shape],
        scratch_shapes=[pltpu.VMEM((tm + SUBLANES, rec_width), F32)],
        compiler_params=pltpu.CompilerParams(
            dimension_semantics=("arbitrary",), vmem_limit_bytes=VMEM_LIMIT),
        name="in_proj",
    )(x2d, g.reshape(1, d), w_in.astype(BF16), cos_t, sin_t, conv_w, rec_row(conv_b), wg,
      rec_row(b_a), rec_row(b_x), rec_row(lam))


def _moba_kernel(q_ref, k_ref, v_ref, o_ref, vt_scr, kmean_scr, s_scr, p_scr, *, nb):
    blk = MOBA_BLOCK
    scale_l2 = (HEAD_DIM ** -0.5) * LOG2E
    tb = (((1,), (1,)), ((), ()))

    ones_row = lax.broadcasted_iota(jnp.int32, (vt_scr.shape[0] - HEAD_DIM, vt_scr.shape[1]), 0) == 0
    vt_scr[HEAD_DIM:, :] = jnp.where(ones_row, 1.0, 0.0).astype(BF16)
    for n in range(nb):
        rows = slice(n * blk, (n + 1) * blk)
        vt_scr[:HEAD_DIM, rows] = v_ref[rows, :].astype(F32).T.astype(BF16)
        kmean_scr[n:n + 1, :] = jnp.mean(k_ref[rows, :].astype(F32), axis=0, keepdims=True)

    km = kmean_scr[...]
    km_hi = km.astype(BF16)
    km_lo = (km - km_hi.astype(F32)).astype(BF16)
    blk_id = lax.broadcasted_iota(jnp.int32, (nb, blk), 0)
    key_pos = lax.broadcasted_iota(jnp.int32, (blk, blk), 0)
    qry_pos = lax.broadcasted_iota(jnp.int32, (blk, blk), 1)
    causal = key_pos <= qry_pos

    def scores(qi):
        slot = qi % 2
        q = q_ref[qi * blk:(qi + 1) * blk, :]

        if qi <= MOBA_TOPK:
            bias = None
        else:
            gate = (lax.dot_general(km_hi, q, tb, preferred_element_type=F32)
                    + lax.dot_general(km_lo, q, tb, preferred_element_type=F32))
            eligible = blk_id < qi
            rem = jnp.where(eligible, gate, -jnp.inf)
            sel = jnp.zeros((nb, blk), jnp.bool_)
            for _t in range(MOBA_TOPK):
                mx = jnp.max(rem, axis=0, keepdims=True)
                first = jnp.min(jnp.where(rem == mx, blk_id, nb), axis=0, keepdims=True)
                pick = blk_id == first
                sel = jnp.logical_or(sel, pick)
                rem = jnp.where(pick, -jnp.inf, rem)
            bias = jnp.where(jnp.logical_and(sel, eligible), 0.0, NEG_BIG)

        m = None
        for n in range(qi + 1):
            rows = slice(n * blk, (n + 1) * blk)
            raw = lax.dot_general(k_ref[rows, :], q, tb, preferred_element_type=F32)
            if n == qi:
                raw = jnp.where(causal, raw, NEG_BIG)
            s_scr[slot, rows, :] = raw
            cm = jnp.max(raw, axis=0, keepdims=True) * scale_l2
            if bias is not None and n < qi:
                cm = cm + bias[n:n + 1, :]
            m = cm if m is None else jnp.maximum(m, cm)
        return bias, m

    def probabilities(qi, bias, m):
        slot = qi % 2
        pace = None
        for n in range(qi + 1):
            shift = -m
            if bias is not None and n < qi:
                shift = bias[n:n + 1, :] - m
            for r0 in range(n * blk, (n + 1) * blk, PACE_ROWS):
                rows = slice(r0, r0 + PACE_ROWS)
                sh = shift if pace is None else shift + pace
                x = s_scr[slot, rows, :] * scale_l2 + sh
                p = jnp.exp2(x.astype(BF16))
                p_scr[slot, rows, :] = p
                last = p[PACE_ROWS - 2 * SUBLANES:, :].astype(F32)
                pace = jnp.where(jnp.min(last, axis=0, keepdims=True) < 0.0, 1.0, 0.0)

    def weighted_values(qi):
        slot = qi % 2
        nk = (qi + 1) * blk
        acc = jnp.dot(vt_scr[:, 0:nk], p_scr[slot, 0:nk, :], preferred_element_type=F32)
        l = acc[HEAD_DIM:HEAD_DIM + 1, :]
        o_ref[qi * blk:(qi + 1) * blk, :] = (acc[:HEAD_DIM, :] / l).T

    state = scores(0)
    for qi in range(nb):
        probabilities(qi, *state)
        state = scores(qi + 1) if qi + 1 < nb else None
        weighted_values(qi)


def _moba(qkv, batch, seq):
    t = qkv.shape[0]
    nb = seq // MOBA_BLOCK
    nh = N_ATTN_HEADS
    return pl.pallas_call(
        functools.partial(_moba_kernel, nb=nb),
        grid=(batch, nh),
        in_specs=[
            pl.BlockSpec((seq, HEAD_DIM), lambda b, h: (b, h)),
            pl.BlockSpec((seq, HEAD_DIM), lambda b, h: (b, nh + h)),
            pl.BlockSpec((seq, HEAD_DIM), lambda b, h: (b, 2 * nh + h)),
        ],
        out_specs=pl.BlockSpec((seq, HEAD_DIM), lambda b, h: (b, h)),
        out_shape=jax.ShapeDtypeStruct((t, nh * HEAD_DIM), F32),
        scratch_shapes=[
            pltpu.VMEM((HEAD_DIM + 2 * SUBLANES, seq), BF16),
            pltpu.VMEM((nb, HEAD_DIM), F32),
            pltpu.VMEM((2, seq, MOBA_BLOCK), F32),
            pltpu.VMEM((2, seq, MOBA_BLOCK), BF16),
        ],
        compiler_params=pltpu.CompilerParams(
            dimension_semantics=("parallel", "parallel"), vmem_limit_bytes=VMEM_LIMIT),
        name="moba",
    )(qkv, qkv, qkv)


def _out_proj_kernel(x_ref, attn_ref, a_ref, b_ref, gg_ref, ga_ref, gr_ref, w_ref, pg_ref, o_ref, hcar,
                     *, s_tiles):
    i = pl.program_id(0)
    tm = x_ref.shape[0]

    @pl.when(i % s_tiles == 0)
    def _():
        hcar[...] = jnp.zeros_like(hcar)

    a = a_ref[...]
    b = b_ref[...]
    row8 = lax.broadcasted_iota(jnp.int32, a.shape, 0) & (SUBLANES - 1)
    for s in (1, 2, 4):
        ar = pltpu.roll(a, s, 0)
        br = pltpu.roll(b, s, 0)
        keep = row8 >= s
        b = jnp.where(keep, a * br + b, b)
        a = jnp.where(keep, a * ar, a)
    carry = hcar[...]
    groups = []
    for r0 in range(0, tm, SUBLANES):
        hg = a[r0:r0 + SUBLANES, :] * carry + b[r0:r0 + SUBLANES, :]
        groups.append(hg)
        carry = hg[SUBLANES - 1:SUBLANES, :]
    hcar[...] = carry
    rec = jnp.concatenate(groups, axis=0) * gg_ref[...]

    at = _rms(attn_ref[...], ga_ref[...]).astype(BF16)
    rc = _rms(rec, gr_ref[...]).astype(BF16)
    aw = at.shape[1]
    y = (jnp.dot(at, w_ref[:aw, :], preferred_element_type=F32)
         + jnp.dot(rc, w_ref[aw:, :], preferred_element_type=F32))
    o_ref[...] = x_ref[...] + _rms(y, pg_ref[...])


def _out_proj(x2d, attn, a, b, gg, ga, gr, w_out, pg, seq, *, tm=512):
    t, d = x2d.shape
    aw = attn.shape[1]
    rw = a.shape[1]
    rec_blk = pl.BlockSpec((tm, rw), lambda i: (i, 0))
    return pl.pallas_call(
        functools.partial(_out_proj_kernel, s_tiles=seq // tm),
        grid=(t // tm,),
        in_specs=[
            pl.BlockSpec((tm, d), lambda i: (i, 0)),
            pl.BlockSpec((tm, aw), lambda i: (i, 0)),
            rec_blk, rec_blk, rec_blk,
            pl.BlockSpec((1, aw), lambda i: (0, 0)),
            pl.BlockSpec((1, rw), lambda i: (0, 0)),
            pl.BlockSpec((aw + rw, d), lambda i: (0, 0), pipeline_mode=pl.Buffered(1)),
            pl.BlockSpec((1, d), lambda i: (0, 0)),
        ],
        out_specs=pl.BlockSpec((tm, d), lambda i: (i, 0)),
        out_shape=jax.ShapeDtypeStruct((t, d), F32),
        scratch_shapes=[pltpu.VMEM((1, rw), F32)],
        compiler_params=pltpu.CompilerParams(
            dimension_semantics=("arbitrary",), vmem_limit_bytes=VMEM_LIMIT),
        name="out_proj",
    )(x2d, attn, a, b, gg, ga.reshape(1, aw), gr.reshape(1, rw), w_out.astype(BF16), pg.reshape(1, d))


def kernel(x, ffn1_pre_g, ffn1_w_gate_up, ffn1_w_down, ffn1_post_g, mix_pre_g, w_in, conv_w, conv_b, rg_w_a, rg_b_a, rg_w_x, rg_b_x, rg_lambda, attn_out_g, rec_out_g, w_out, mix_post_g, ffn2_pre_g, ffn2_w_gate_up, ffn2_w_down, ffn2_post_g):
    batch, seq, d = x.shape
    depth = ffn1_pre_g.shape[0]
    attn_width = N_ATTN_HEADS * HEAD_DIM
    rec_width = conv_w.shape[-1]
    assert w_in.shape[-1] == 3 * attn_width + 2 * rec_width
    assert seq % 512 == 0 and seq % MOBA_BLOCK == 0
    h = x.reshape(batch * seq, d)
    for l in range(depth):
        h = _ffn(h, ffn1_pre_g[l], ffn1_w_gate_up[l], ffn1_w_down[l], ffn1_post_g[l])
        qkv, a, b, gg = _in_proj(h, mix_pre_g[l], w_in[l], conv_w[l], conv_b[l], rg_w_a[l], rg_b_a[l],
                                 rg_w_x[l], rg_b_x[l], rg_lambda[l], seq, attn_width, rec_width)
        attn = _moba(qkv, batch, seq)
        h = _out_proj(h, attn, a, b, gg, attn_out_g[l], rec_out_g[l], w_out[l], mix_post_g[l], seq)
        h = _ffn(h, ffn2_pre_g[l], ffn2_w_gate_up[l], ffn2_w_down[l], ffn2_post_g[l])
    return h.reshape(batch, seq, d)
```

```python
import functools

import jax
import jax.numpy as jnp
import numpy as np
from jax import lax
from jax.experimental import pallas as pl
from jax.experimental.pallas import tpu as pltpu

F32 = jnp.float32
BF16 = jnp.bfloat16

N_ATTN_HEADS = 8
HEAD_DIM = 128
ROT_DIM = 32
ROPE_THETA = 500000.0
MOBA_BLOCK = 256
MOBA_TOPK = 3
CONV_WIDTH = 4
RGLRU_C = 8.0
FFN_HALF = 0.5
EPS = 1e-6

VMEM_LIMIT = 56 * 1024 * 1024
LANES = 128
SUBLANES = 8

NEG_BIG = -1e30
PACE_ROWS = 128
LOG2E = 1.4426950408889634


def _rms(x, g):
    ms = jnp.mean(x * x, axis=-1, keepdims=True)
    return x * lax.rsqrt(ms + EPS) * g


def _ffn_steps(j, last, x_ref, preg_ref, postg_ref, o_ref, h_scr, weights, row_blk):
    row_blocks = [slice(r, r + row_blk) for r in range(0, x_ref.shape[0], row_blk)]

    def swiglu_down(h, w):
        wg, wu, wd = w
        g = jnp.dot(h, wg, preferred_element_type=F32)
        u = jnp.dot(h, wu, preferred_element_type=F32)
        a = (g * jax.nn.sigmoid(g) * u).astype(BF16)
        return jnp.dot(a, wd, preferred_element_type=F32)

    @pl.when(j == 0)
    def _():
        w = weights(False)
        for rows in row_blocks:
            h = _rms(x_ref[rows, :], preg_ref[...]).astype(BF16)
            h_scr[rows, :] = h
            o_ref[rows, :] = swiglu_down(h, w)

    @pl.when(jnp.logical_and(j > 0, j < last))
    def _():
        o_ref[...] += swiglu_down(h_scr[...], weights(False))

    @pl.when(j == last)
    def _():
        w = weights(True)
        paces = []
        for r, rows in enumerate(row_blocks):
            h = h_scr[rows, :]
            if r >= 2:
                h = h + paces[r - 2]
            acc = o_ref[rows, :] + swiglu_down(h, w)
            out = x_ref[rows, :] + FFN_HALF * _rms(acc, postg_ref[...])
            o_ref[rows, :] = out
            tail = jnp.max(out[row_blk - SUBLANES:, :], axis=0, keepdims=True)
            paces.append(jnp.where(tail > jnp.inf, 1.0, 0.0).astype(BF16))


def _ffn_head_kernel(x_ref, preg_ref, wg0_ref, wg1_ref, wu0_ref, wu1_ref, wd0_ref, wd1_ref, postg_ref,
                     o_ref, cg_ref, cu_ref, cd_ref, h_scr, *, n_last, row_blk):
    j = pl.program_id(0)

    def weights(is_last):
        n = n_last if is_last else 2
        pick = lambda refs, axis: jnp.concatenate(
            [r[...].astype(BF16) if k < n else jnp.zeros(r.shape, BF16) for k, r in enumerate(refs)],
            axis=axis)
        w = pick([wg0_ref, wg1_ref], 1), pick([wu0_ref, wu1_ref], 1), pick([wd0_ref, wd1_ref], 0)
        cg_ref[...], cu_ref[...], cd_ref[...] = w
        return w

    _ffn_steps(j, pl.num_programs(0) - 1, x_ref, preg_ref, postg_ref, o_ref, h_scr, weights, row_blk)


def _ffn_tail_kernel(x_ref, preg_ref, wg_ref, wu_ref, wd_ref, postg_ref, head_out_ref, o_ref, h_scr,
                     *, row_blk):
    del head_out_ref
    weights = lambda is_last: (wg_ref[...], wu_ref[...], wd_ref[...])
    _ffn_steps(pl.program_id(1), pl.num_programs(1) - 1, x_ref, preg_ref, postg_ref, o_ref, h_scr,
               weights, row_blk)


def _ffn(x2d, pre_g, w_gate_up, w_down, post_g, *, tm=1024, tf=512, row_blk=256):
    t, d = x2d.shape
    f = w_down.shape[0]
    assert f % LANES == 0 and tf % (2 * LANES) == 0
    nblk = f // LANES
    fp = -(-f // tf) * tf
    n_head = fp // (2 * LANES)
    n_last = nblk - 2 * (n_head - 1)
    assert 1 <= n_last <= 2
    pre = pre_g.reshape(1, d)
    post = post_g.reshape(1, d)
    vec = pl.BlockSpec((1, d), lambda *idx: (0, 0))
    col = lambda k, base: pl.BlockSpec(
        (d, LANES), lambda j: (0, base + jnp.minimum(2 * j + k, nblk - 1)))
    row = lambda k: pl.BlockSpec((LANES, d), lambda j: (jnp.minimum(2 * j + k, nblk - 1), 0))
    once = pl.Buffered(1)
    head_out, cg, cu, cd = pl.pallas_call(
        functools.partial(_ffn_head_kernel, n_last=n_last, row_blk=row_blk),
        grid=(n_head,),
        in_specs=[
            pl.BlockSpec((tm, d), lambda j: (0, 0), pipeline_mode=once),
            vec,
            col(0, 0), col(1, 0), col(0, nblk), col(1, nblk),
            row(0), row(1),
            vec,
        ],
        out_specs=[
            pl.BlockSpec((tm, d), lambda j: (0, 0)),
            pl.BlockSpec((d, 2 * LANES), lambda j: (0, j)),
            pl.BlockSpec((d, 2 * LANES), lambda j: (0, j)),
            pl.BlockSpec((2 * LANES, d), lambda j: (j, 0)),
        ],
        out_shape=[
            jax.ShapeDtypeStruct((t, d), F32),
            jax.ShapeDtypeStruct((d, fp), BF16),
            jax.ShapeDtypeStruct((d, fp), BF16),
            jax.ShapeDtypeStruct((fp, d), BF16),
        ],
        scratch_shapes=[pltpu.VMEM((tm, d), BF16)],
        compiler_params=pltpu.CompilerParams(
            dimension_semantics=("arbitrary",), vmem_limit_bytes=VMEM_LIMIT),
        name="ffn_head",
    )(x2d, pre, w_gate_up, w_gate_up, w_gate_up, w_gate_up, w_down, w_down, post)
    if t == tm:
        return head_out
    return pl.pallas_call(
        functools.partial(_ffn_tail_kernel, row_blk=row_blk),
        grid=(t // tm - 1, fp // tf),
        in_specs=[
            pl.BlockSpec((tm, d), lambda i, j: (i + 1, 0)),
            vec,
            pl.BlockSpec((d, tf), lambda i, j: (0, j)),
            pl.BlockSpec((d, tf), lambda i, j: (0, j)),
            pl.BlockSpec((tf, d), lambda i, j: (j, 0)),
            vec,
            pl.BlockSpec(memory_space=pl.ANY),
        ],
        out_specs=pl.BlockSpec((tm, d), lambda i, j: (i + 1, 0)),
        out_shape=jax.ShapeDtypeStruct((t, d), F32),
        input_output_aliases={6: 0},
        scratch_shapes=[pltpu.VMEM((tm, d), BF16)],
        compiler_params=pltpu.CompilerParams(
            dimension_semantics=("parallel", "arbitrary"), vmem_limit_bytes=VMEM_LIMIT),
        name="ffn_tail",
    )(x2d, pre, cg, cu, cd, post, head_out)


def _in_proj_kernel(x_ref, g_ref, w_ref, cos_ref, sin_ref, cw_ref, cb_ref, wg_ref, ba_ref, bx_ref,
                    lam_ref, qkv_ref, a_ref, b_ref, gg_ref, xbuf,
                    *, tn, attn_width, rec_width, gate_tile, s_tiles):
    i = pl.program_id(0)
    tm = x_ref.shape[0]
    pad = SUBLANES

    @pl.when(i % s_tiles == 0)
    def _():
        xbuf[0:pad, :] = jnp.zeros((pad, rec_width), F32)

    h = _rms(x_ref[...], g_ref[...]).astype(BF16)
    c = cos_ref[...]
    s = sin_ref[...]
    lane = lax.broadcasted_iota(jnp.int32, c.shape, 1)
    first_half = lane < ROT_DIM // 2

    def proj(col0):
        return jnp.dot(h, w_ref[:, col0:col0 + tn], preferred_element_type=F32)

    def emit_qk(col0):
        y = proj(col0)
        for hh in range(tn // HEAD_DIM):
            yh = y[:, hh * HEAD_DIM:(hh + 1) * HEAD_DIM]
            partner = jnp.where(first_half,
                                pltpu.roll(yh, HEAD_DIM - ROT_DIM // 2, 1),
                                pltpu.roll(yh, ROT_DIM // 2, 1))
            col = col0 + hh * HEAD_DIM
            qkv_ref[:, col:col + HEAD_DIM] = (yh * c + partner * s).astype(BF16)

    def emit_v(col0):
        qkv_ref[:, col0:col0 + tn] = proj(col0).astype(BF16)

    def emit_gate_gelu(col0):
        gg_ref[:, col0:col0 + tn] = jax.nn.gelu(proj(3 * attn_width + rec_width + col0))

    w = cw_ref[...]

    def emit_recurrence_inputs(ct):
        cols = slice(ct * gate_tile, (ct + 1) * gate_tile)
        xc = cb_ref[:, cols] + w[CONV_WIDTH - 1:CONV_WIDTH, cols] * xbuf[pad:pad + tm, cols]
        for j in range(1, CONV_WIDTH):
            xc = xc + w[CONV_WIDTH - 1 - j:CONV_WIDTH - j, cols] * xbuf[pad - j:pad - j + tm, cols]
        gates = jnp.dot(xc.astype(BF16), wg_ref[ct], preferred_element_type=F32)
        r = jax.nn.sigmoid(gates[:, :gate_tile] + ba_ref[:, cols])
        gi = jax.nn.sigmoid(gates[:, gate_tile:] + bx_ref[:, cols])
        z = -lam_ref[:, cols]
        softplus = jnp.maximum(z, 0.0) + jnp.log1p(jnp.exp(-jnp.abs(z)))
        log_a = (-RGLRU_C * r) * softplus
        a = jnp.exp(log_a)
        a_ref[:, cols] = a
        b_ref[:, cols] = jnp.sqrt(-jnp.tanh(log_a) * (a * a + 1.0)) * (gi * xc)

    for col0 in range(0, rec_width, tn):
        xbuf[pad:pad + tm, col0:col0 + tn] = proj(3 * attn_width + col0)
    matmul_work = ([functools.partial(emit_qk, c0) for c0 in range(0, 2 * attn_width, tn)]
                   + [functools.partial(emit_v, c0) for c0 in range(2 * attn_width, 3 * attn_width, tn)]
                   + [functools.partial(emit_gate_gelu, c0) for c0 in range(0, rec_width, tn)])
    n_gate_tiles = rec_width // gate_tile
    per = -(-len(matmul_work) // n_gate_tiles)
    for ct in range(n_gate_tiles):
        emit_recurrence_inputs(ct)
        for work in matmul_work[ct * per:(ct + 1) * per]:
            work()
    for work in matmul_work[n_gate_tiles * per:]:
        work()
    xbuf[0:pad, :] = xbuf[tm:tm + pad, :]


def _rope_tables(seq):
    inv_freq = ROPE_THETA ** (-np.arange(0, ROT_DIM, 2, dtype=np.float64) / ROT_DIM)
    ang = np.arange(seq, dtype=np.float64)[:, None] * inv_freq[None, :]
    cos, sin = np.cos(ang), np.sin(ang)
    ones = np.ones((seq, HEAD_DIM - ROT_DIM))
    zeros = np.zeros((seq, HEAD_DIM - ROT_DIM))
    cos_t = np.concatenate([cos, cos, ones], axis=1)
    sin_t = np.concatenate([-sin, sin, zeros], axis=1)
    return jnp.asarray(cos_t, F32), jnp.asarray(sin_t, F32)


def _block_diag_gates(w_a, w_x, cw):
    nblk, bd, _ = w_a.shape
    per = cw // bd
    nt = nblk // per
    eye = jnp.eye(per, dtype=w_a.dtype)

    def bdiag(w):
        w = w.reshape(nt, per, bd, bd)
        full = jnp.einsum("tpij,pq->tpiqj", w, eye)
        return full.reshape(nt, cw, cw)

    return jnp.concatenate([bdiag(w_a), bdiag(w_x)], axis=2).astype(BF16)


def _in_proj(x2d, g, w_in, conv_w, conv_b, w_a, b_a, w_x, b_x, lam, seq, attn_width, rec_width,
             *, tm=512, tn=512, gate_tile=128):
    t, d = x2d.shape
    cols = w_in.shape[1]
    s_tiles = seq // tm
    cos_t, sin_t = _rope_tables(seq)
    wg = _block_diag_gates(w_a, w_x, gate_tile)
    kern = functools.partial(_in_proj_kernel, tn=tn, attn_width=attn_width, rec_width=rec_width,
                             gate_tile=gate_tile, s_tiles=s_tiles)
    whole = lambda shape: pl.BlockSpec(shape, lambda i: (0,) * len(shape))
    rec_row = lambda v: v.reshape(1, rec_width)
    rec_out = pl.BlockSpec((tm, rec_width), lambda i: (i, 0))
    rec_shape = jax.ShapeDtypeStruct((t, rec_width), F32)
    return pl.pallas_call(
        kern,
        grid=(t // tm,),
        in_specs=[
            pl.BlockSpec((tm, d), lambda i: (i, 0)),
            whole((1, d)),
            pl.BlockSpec((d, cols), lambda i: (0, 0), pipeline_mode=pl.Buffered(1)),
            pl.BlockSpec((tm, HEAD_DIM), lambda i: (i % s_tiles, 0)),
            pl.BlockSpec((tm, HEAD_DIM), lambda i: (i % s_tiles, 0)),
            whole((CONV_WIDTH, rec_width)),
            whole((1, rec_width)),
            whole(wg.shape),
            whole((1, rec_width)), whole((1, rec_width)), whole((1, rec_width)),
        ],
        out_specs=[pl.BlockSpec((tm, 3 * attn_width), lambda i: (i, 0)), rec_out, rec_out, rec_out],
        out_shape=[jax.ShapeDtypeStruct((t, 3 * attn_width), BF16), rec_shape, rec_shape, rec_shape],
        scratch_shapes=[pltpu.VMEM((tm + SUBLANES, rec_width), F32)],
        compiler_params=pltpu.CompilerParams(
            dimension_semantics=("arbitrary",), vmem_limit_bytes=VMEM_LIMIT),
        name="in_proj",
    )(x2d, g.reshape(1, d), w_in.astype(BF16), cos_t, sin_t, conv_w, rec_row(conv_b), wg,
      rec_row(b_a), rec_row(b_x), rec_row(lam))


def _moba_kernel(q_ref, k_ref, v_ref, o_ref, vt_scr, kmean_scr, s_scr, p_scr, *, nb):
    blk = MOBA_BLOCK
    scale_l2 = (HEAD_DIM ** -0.5) * LOG2E
    tb = (((1,), (1,)), ((), ()))

    ones_row = lax.broadcasted_iota(jnp.int32, (vt_scr.shape[0] - HEAD_DIM, vt_scr.shape[1]), 0) == 0
    vt_scr[HEAD_DIM:, :] = jnp.where(ones_row, 1.0, 0.0).astype(BF16)
    for n in range(nb):
        rows = slice(n * blk, (n + 1) * blk)
        vt_scr[:HEAD_DIM, rows] = v_ref[rows, :].astype(F32).T.astype(BF16)
        kmean_scr[n:n + 1, :] = jnp.mean(k_ref[rows, :].astype(F32), axis=0, keepdims=True)

    km = kmean_scr[...]
    km_hi = km.astype(BF16)
    km_lo = (km - km_hi.astype(F32)).astype(BF16)
    blk_id = lax.broadcasted_iota(jnp.int32, (nb, blk), 0)
    key_pos = lax.broadcasted_iota(jnp.int32, (blk, blk), 0)
    qry_pos = lax.broadcasted_iota(jnp.int32, (blk, blk), 1)
    causal = key_pos <= qry_pos

    def scores(qi):
        slot = qi % 2
        q = q_ref[qi * blk:(qi + 1) * blk, :]

        if qi <= MOBA_TOPK:
            bias = None
        else:
            gate = (lax.dot_general(km_hi, q, tb, preferred_element_type=F32)
                    + lax.dot_general(km_lo, q, tb, preferred_element_type=F32))
            eligible = blk_id < qi
            rem = jnp.where(eligible, gate, -jnp.inf)
            sel = jnp.zeros((nb, blk), jnp.bool_)
            for _t in range(MOBA_TOPK):
                mx = jnp.max(rem, axis=0, keepdims=True)
                first = jnp.min(jnp.where(rem == mx, blk_id, nb), axis=0, keepdims=True)
                pick = blk_id == first
                sel = jnp.logical_or(sel, pick)
                rem = jnp.where(pick, -jnp.inf, rem)
            bias = jnp.where(jnp.logical_and(sel, eligible), 0.0, NEG_BIG)

        m = None
        for n in range(qi + 1):
            rows = slice(n * blk, (n + 1) * blk)
            raw = lax.dot_general(k_ref[rows, :], q, tb, preferred_element_type=F32)
            if n == qi:
                raw = jnp.where(causal, raw, NEG_BIG)
            s_scr[slot, rows, :] = raw
            cm = jnp.max(raw, axis=0, keepdims=True) * scale_l2
            if bias is not None and n < qi:
                cm = cm + bias[n:n + 1, :]
            m = cm if m is None else jnp.maximum(m, cm)
        return bias, m

    def probabilities(qi, bias, m):
        slot = qi % 2
        pace = None
        for n in range(qi + 1):
            shift = -m
            if bias is not None and n < qi:
                shift = bias[n:n + 1, :] - m
            for r0 in range(n * blk, (n + 1) * blk, PACE_ROWS):
                rows = slice(r0, r0 + PACE_ROWS)
                sh = shift if pace is None else shift + pace
                x = s_scr[slot, rows, :] * scale_l2 + sh
                p = jnp.exp2(x.astype(BF16))
                p_scr[slot, rows, :] = p
                last = p[PACE_ROWS - 2 * SUBLANES:, :].astype(F32)
                pace = jnp.where(jnp.min(last, axis=0, keepdims=True) < 0.0, 1.0, 0.0)

    def weighted_values(qi):
        slot = qi % 2
        nk = (qi + 1) * blk
        acc = jnp.dot(vt_scr[:, 0:nk], p_scr[slot, 0:nk, :], preferred_element_type=F32)
        l = acc[HEAD_DIM:HEAD_DIM + 1, :]
        o_ref[qi * blk:(qi + 1) * blk, :] = (acc[:HEAD_DIM, :] / l).T

    state = scores(0)
    for qi in range(nb):
        probabilities(qi, *state)
        state = scores(qi + 1) if qi + 1 < nb else None
        weighted_values(qi)


def _moba(qkv, batch, seq):
    t = qkv.shape[0]
    nb = seq // MOBA_BLOCK
    nh = N_ATTN_HEADS
    return pl.pallas_call(
        functools.partial(_moba_kernel, nb=nb),
        grid=(batch, nh),
        in_specs=[
            pl.BlockSpec((seq, HEAD_DIM), lambda b, h: (b, h)),
            pl.BlockSpec((seq, HEAD_DIM), lambda b, h: (b, nh + h)),
            pl.BlockSpec((seq, HEAD_DIM), lambda b, h: (b, 2 * nh + h)),
        ],
        out_specs=pl.BlockSpec((seq, HEAD_DIM), lambda b, h: (b, h)),
        out_shape=jax.ShapeDtypeStruct((t, nh * HEAD_DIM), F32),
        scratch_shapes=[
            pltpu.VMEM((HEAD_DIM + 2 * SUBLANES, seq), BF16),
            pltpu.VMEM((nb, HEAD_DIM), F32),
            pltpu.VMEM((2, seq, MOBA_BLOCK), F32),
            pltpu.VMEM((2, seq, MOBA_BLOCK), BF16),
        ],
        compiler_params=pltpu.CompilerParams(
            dimension_semantics=("parallel", "parallel"), vmem_limit_bytes=VMEM_LIMIT),
        name="moba",
    )(qkv, qkv, qkv)


def _out_proj_kernel(x_ref, attn_ref, a_ref, b_ref, gg_ref, ga_ref, gr_ref, w_ref, pg_ref, o_ref, hcar,
                     *, s_tiles):
    i = pl.program_id(0)
    tm = x_ref.shape[0]

    @pl.when(i % s_tiles == 0)
    def _():
        hcar[...] = jnp.zeros_like(hcar)

    a = a_ref[...]
    b = b_ref[...]
    row8 = lax.broadcasted_iota(jnp.int32, a.shape, 0) & (SUBLANES - 1)
    for s in (1, 2, 4):
        ar = pltpu.roll(a, s, 0)
        br = pltpu.roll(b, s, 0)
        keep = row8 >= s
        b = jnp.where(keep, a * br + b, b)
        a = jnp.where(keep, a * ar, a)
    carry = hcar[...]
    groups = []
    for r0 in range(0, tm, SUBLANES):
        hg = a[r0:r0 + SUBLANES, :] * carry + b[r0:r0 + SUBLANES, :]
        groups.append(hg)
        carry = hg[SUBLANES - 1:SUBLANES, :]
    hcar[...] = carry
    rec = jnp.concatenate(groups, axis=0) * gg_ref[...]

    at = _rms(attn_ref[...], ga_ref[...]).astype(BF16)
    rc = _rms(rec, gr_ref[...]).astype(BF16)
    aw = at.shape[1]
    y = (jnp.dot(at, w_ref[:aw, :], preferred_element_type=F32)
         + jnp.dot(rc, w_ref[aw:, :], preferred_element_type=F32))
    o_ref[...] = x_ref[...] + _rms(y, pg_ref[...])


def _out_proj(x2d, attn, a, b, gg, ga, gr, w_out, pg, seq, *, tm=512):
    t, d = x2d.shape
    aw = attn.shape[1]
    rw = a.shape[1]
    rec_blk = pl.BlockSpec((tm, rw), lambda i: (i, 0))
    return pl.pallas_call(
        functools.partial(_out_proj_kernel, s_tiles=seq // tm),
        grid=(t // tm,),
        in_specs=[
            pl.BlockSpec((tm, d), lambda i: (i, 0)),
            pl.BlockSpec((tm, aw), lambda i: (i, 0)),
            rec_blk, rec_blk, rec_blk,
            pl.BlockSpec((1, aw), lambda i: (0, 0)),
            pl.BlockSpec((1, rw), lambda i: (0, 0)),
            pl.BlockSpec((aw + rw, d), lambda i: (0, 0), pipeline_mode=pl.Buffered(1)),
            pl.BlockSpec((1, d), lambda i: (0, 0)),
        ],
        out_specs=pl.BlockSpec((tm, d), lambda i: (i, 0)),
        out_shape=jax.ShapeDtypeStruct((t, d), F32),
        scratch_shapes=[pltpu.VMEM((1, rw), F32)],
        compiler_params=pltpu.CompilerParams(
            dimension_semantics=("arbitrary",), vmem_limit_bytes=VMEM_LIMIT),
        name="out_proj",
    )(x2d, attn, a, b, gg, ga.reshape(1, aw), gr.reshape(1, rw), w_out.astype(BF16), pg.reshape(1, d))


def kernel(x, ffn1_pre_g, ffn1_w_gate_up, ffn1_w_down, ffn1_post_g, mix_pre_g, w_in, conv_w, conv_b, rg_w_a, rg_b_a, rg_w_x, rg_b_x, rg_lambda, attn_out_g, rec_out_g, w_out, mix_post_g, ffn2_pre_g, ffn2_w_gate_up, ffn2_w_down, ffn2_post_g):
    batch, seq, d = x.shape
    depth = ffn1_pre_g.shape[0]
    attn_width = N_ATTN_HEADS * HEAD_DIM
    rec_width = conv_w.shape[-1]
    assert w_in.shape[-1] == 3 * attn_width + 2 * rec_width
    assert seq % 512 == 0 and seq % MOBA_BLOCK == 0
    h = x.reshape(batch * seq, d)
    for l in range(depth):
        h = _ffn(h, ffn1_pre_g[l], ffn1_w_gate_up[l], ffn1_w_down[l], ffn1_post_g[l])
        qkv, a, b, gg = _in_proj(h, mix_pre_g[l], w_in[l], conv_w[l], conv_b[l], rg_w_a[l], rg_b_a[l],
                                 rg_w_x[l], rg_b_x[l], rg_lambda[l], seq, attn_width, rec_width)
        attn = _moba(qkv, batch, seq)
        h = _out_proj(h, attn, a, b, gg, attn_out_g[l], rec_out_g[l], w_out[l], mix_post_g[l], seq)
        h = _ffn(h, ffn2_pre_g[l], ffn2_w_gate_up[l], ffn2_w_down[l], ffn2_post_g[l])
    return h.reshape(batch, seq, d)
```

```python
import functools

import jax
import jax.numpy as jnp
import numpy as np
from jax import lax
from jax.experimental import pallas as pl
from jax.experimental.pallas import tpu as pltpu

F32 = jnp.float32
BF16 = jnp.bfloat16

N_ATTN_HEADS = 8
HEAD_DIM = 128
ROT_DIM = 32
ROPE_THETA = 500000.0
MOBA_BLOCK = 256
MOBA_TOPK = 3
CONV_WIDTH = 4
RGLRU_C = 8.0
FFN_HALF = 0.5
EPS = 1e-6

VMEM_LIMIT = 56 * 1024 * 1024
LANES = 128
SUBLANES = 8

NEG_BIG = -1e30
PACE_ROWS = 128
LOG2E = 1.4426950408889634


def _rms(x, g):
    ms = jnp.mean(x * x, axis=-1, keepdims=True)
    return x * lax.rsqrt(ms + EPS) * g


def _ffn_steps(j, last, x_ref, preg_ref, postg_ref, o_ref, h_scr, weights, row_blk):
    row_blocks = [slice(r, r + row_blk) for r in range(0, x_ref.shape[0], row_blk)]

    def swiglu_down(h, w):
        wg, wu, wd = w
        g = jnp.dot(h, wg, preferred_element_type=F32)
        u = jnp.dot(h, wu, preferred_element_type=F32)
        a = (g * jax.nn.sigmoid(g) * u).astype(BF16)
        return jnp.dot(a, wd, preferred_element_type=F32)

    @pl.when(j == 0)
    def _():
        w = weights(False)
        for rows in row_blocks:
            h = _rms(x_ref[rows, :], preg_ref[...]).astype(BF16)
            h_scr[rows, :] = h
            o_ref[rows, :] = swiglu_down(h, w)

    @pl.when(jnp.logical_and(j > 0, j < last))
    def _():
        wg, wu, wd = weights(False)
        h = h_scr[...]
        n_split = 2 if wg.shape[1] % (4 * LANES) == 0 else 1
        part = wg.shape[1] // n_split
        for c0 in range(0, wg.shape[1], part):
            o_ref[...] += swiglu_down(h, (wg[:, c0:c0 + part], wu[:, c0:c0 + part], wd[c0:c0 + part, :]))

    @pl.when(j == last)
    def _():
        w = weights(True)
        paces = []
        for r, rows in enumerate(row_blocks):
            h = h_scr[rows, :]
            if r >= 2:
                h = h + paces[r - 2]
            acc = o_ref[rows, :] + swiglu_down(h, w)
            out = x_ref[rows, :] + FFN_HALF * _rms(acc, postg_ref[...])
            o_ref[rows, :] = out
            tail = jnp.max(out[row_blk - SUBLANES:, :], axis=0, keepdims=True)
            paces.append(jnp.where(tail > jnp.inf, 1.0, 0.0).astype(BF16))


def _ffn_head_kernel(x_ref, preg_ref, wg0_ref, wg1_ref, wu0_ref, wu1_ref, wd0_ref, wd1_ref, postg_ref,
                     o_ref, cg_ref, cu_ref, cd_ref, h_scr, *, n_last, row_blk):
    j = pl.program_id(0)

    def weights(is_last):
        n = n_last if is_last else 2
        pick = lambda refs, axis: jnp.concatenate(
            [r[...].astype(BF16) if k < n else jnp.zeros(r.shape, BF16) for k, r in enumerate(refs)],
            axis=axis)
        w = pick([wg0_ref, wg1_ref], 1), pick([wu0_ref, wu1_ref], 1), pick([wd0_ref, wd1_ref], 0)
        cg_ref[...], cu_ref[...], cd_ref[...] = w
        return w

    _ffn_steps(j, pl.num_programs(0) - 1, x_ref, preg_ref, postg_ref, o_ref, h_scr, weights, row_blk)


def _ffn_tail_kernel(x_ref, preg_ref, wg_ref, wu_ref, wd_ref, postg_ref, head_out_ref, o_ref, h_scr,
                     *, row_blk):
    del head_out_ref
    weights = lambda is_last: (wg_ref[...], wu_ref[...], wd_ref[...])
    _ffn_steps(pl.program_id(1), pl.num_programs(1) - 1, x_ref, preg_ref, postg_ref, o_ref, h_scr,
               weights, row_blk)


def _ffn(x2d, pre_g, w_gate_up, w_down, post_g, *, tm=1024, tf=512, row_blk=256):
    t, d = x2d.shape
    f = w_down.shape[0]
    assert f % LANES == 0 and tf % (2 * LANES) == 0
    nblk = f // LANES
    fp = -(-f // tf) * tf
    n_head = fp // (2 * LANES)
    n_last = nblk - 2 * (n_head - 1)
    assert 1 <= n_last <= 2
    pre = pre_g.reshape(1, d)
    post = post_g.reshape(1, d)
    vec = pl.BlockSpec((1, d), lambda *idx: (0, 0))
    col = lambda k, base: pl.BlockSpec(
        (d, LANES), lambda j: (0, base + jnp.minimum(2 * j + k, nblk - 1)))
    row = lambda k: pl.BlockSpec((LANES, d), lambda j: (jnp.minimum(2 * j + k, nblk - 1), 0))
    once = pl.Buffered(1)
    head_out, cg, cu, cd = pl.pallas_call(
        functools.partial(_ffn_head_kernel, n_last=n_last, row_blk=row_blk),
        grid=(n_head,),
        in_specs=[
            pl.BlockSpec((tm, d), lambda j: (0, 0), pipeline_mode=once),
            vec,
            col(0, 0), col(1, 0), col(0, nblk), col(1, nblk),
            row(0), row(1),
            vec,
        ],
        out_specs=[
            pl.BlockSpec((tm, d), lambda j: (0, 0)),
            pl.BlockSpec((d, 2 * LANES), lambda j: (0, j)),
            pl.BlockSpec((d, 2 * LANES), lambda j: (0, j)),
            pl.BlockSpec((2 * LANES, d), lambda j: (j, 0)),
        ],
        out_shape=[
            jax.ShapeDtypeStruct((t, d), F32),
            jax.ShapeDtypeStruct((d, fp), BF16),
            jax.ShapeDtypeStruct((d, fp), BF16),
            jax.ShapeDtypeStruct((fp, d), BF16),
        ],
        scratch_shapes=[pltpu.VMEM((tm, d), BF16)],
        compiler_params=pltpu.CompilerParams(
            dimension_semantics=("arbitrary",), vmem_limit_bytes=VMEM_LIMIT),
        name="ffn_head",
    )(x2d, pre, w_gate_up, w_gate_up, w_gate_up, w_gate_up, w_down, w_down, post)
    if t == tm:
        return head_out
    return pl.pallas_call(
        functools.partial(_ffn_tail_kernel, row_blk=row_blk),
        grid=(t // tm - 1, fp // tf),
        in_specs=[
            pl.BlockSpec((tm, d), lambda i, j: (i + 1, 0)),
            vec,
            pl.BlockSpec((d, tf), lambda i, j: (0, j)),
            pl.BlockSpec((d, tf), lambda i, j: (0, j)),
            pl.BlockSpec((tf, d), lambda i, j: (j, 0)),
            vec,
            pl.BlockSpec(memory_space=pl.ANY),
        ],
        out_specs=pl.BlockSpec((tm, d), lambda i, j: (i + 1, 0)),
        out_shape=jax.ShapeDtypeStruct((t, d), F32),
        input_output_aliases={6: 0},
        scratch_shapes=[pltpu.VMEM((tm, d), BF16)],
        compiler_params=pltpu.CompilerParams(
            dimension_semantics=("parallel", "arbitrary"), vmem_limit_bytes=VMEM_LIMIT),
        name="ffn_tail",
    )(x2d, pre, cg, cu, cd, post, head_out)


def _in_proj_kernel(x_ref, g_ref, w_ref, cos_ref, sin_ref, cw_ref, cb_ref, wg_ref, ba_ref, bx_ref,
                    lam_ref, qkv_ref, a_ref, b_ref, gg_ref, xbuf,
                    *, tn, attn_width, rec_width, gate_tile, s_tiles):
    i = pl.program_id(0)
    tm = x_ref.shape[0]
    pad = SUBLANES

    @pl.when(i % s_tiles == 0)
    def _():
        xbuf[0:pad, :] = jnp.zeros((pad, rec_width), F32)

    h = _rms(x_ref[...], g_ref[...]).astype(BF16)
    c = cos_ref[...]
    s = sin_ref[...]
    lane = lax.broadcasted_iota(jnp.int32, c.shape, 1)
    first_half = lane < ROT_DIM // 2

    def proj(col0):
        return jnp.dot(h, w_ref[:, col0:col0 + tn], preferred_element_type=F32)

    def emit_qk(col0):
        y = proj(col0)
        for hh in range(tn // HEAD_DIM):
            yh = y[:, hh * HEAD_DIM:(hh + 1) * HEAD_DIM]
            partner = jnp.where(first_half,
                                pltpu.roll(yh, HEAD_DIM - ROT_DIM // 2, 1),
                                pltpu.roll(yh, ROT_DIM // 2, 1))
            col = col0 + hh * HEAD_DIM
            qkv_ref[:, col:col + HEAD_DIM] = (yh * c + partner * s).astype(BF16)

    def emit_v(col0):
        qkv_ref[:, col0:col0 + tn] = proj(col0).astype(BF16)

    def emit_gate_gelu(col0):
        gg_ref[:, col0:col0 + tn] = jax.nn.gelu(proj(3 * attn_width + rec_width + col0))

    w = cw_ref[...]

    def emit_recurrence_inputs(ct):
        cols = slice(ct * gate_tile, (ct + 1) * gate_tile)
        xc = cb_ref[:, cols] + w[CONV_WIDTH - 1:CONV_WIDTH, cols] * xbuf[pad:pad + tm, cols]
        for j in range(1, CONV_WIDTH):
            xc = xc + w[CONV_WIDTH - 1 - j:CONV_WIDTH - j, cols] * xbuf[pad - j:pad - j + tm, cols]
        gates = jnp.dot(xc.astype(BF16), wg_ref[ct], preferred_element_type=F32)
        r = jax.nn.sigmoid(gates[:, :gate_tile] + ba_ref[:, cols])
        gi = jax.nn.sigmoid(gates[:, gate_tile:] + bx_ref[:, cols])
        z = -lam_ref[:, cols]
        softplus = jnp.maximum(z, 0.0) + jnp.log1p(jnp.exp(-jnp.abs(z)))
        log_a = (-RGLRU_C * r) * softplus
        a = jnp.exp(log_a)
        a_ref[:, cols] = a
        b_ref[:, cols] = jnp.sqrt(-jnp.tanh(log_a) * (a * a + 1.0)) * (gi * xc)

    for col0 in range(0, rec_width, tn):
        xbuf[pad:pad + tm, col0:col0 + tn] = proj(3 * attn_width + col0)
    matmul_work = ([functools.partial(emit_qk, c0) for c0 in range(0, 2 * attn_width, tn)]
                   + [functools.partial(emit_v, c0) for c0 in range(2 * attn_width, 3 * attn_width, tn)]
                   + [functools.partial(emit_gate_gelu, c0) for c0 in range(0, rec_width, tn)])
    n_gate_tiles = rec_width // gate_tile
    per = -(-len(matmul_work) // n_gate_tiles)
    for ct in range(n_gate_tiles):
        emit_recurrence_inputs(ct)
        for work in matmul_work[ct * per:(ct + 1) * per]:
            work()
    for work in matmul_work[n_gate_tiles * per:]:
        work()
    xbuf[0:pad, :] = xbuf[tm:tm + pad, :]


def _rope_tables(seq):
    inv_freq = ROPE_THETA ** (-np.arange(0, ROT_DIM, 2, dtype=np.float64) / ROT_DIM)
    ang = np.arange(seq, dtype=np.float64)[:, None] * inv_freq[None, :]
    cos, sin = np.cos(ang), np.sin(ang)
    ones = np.ones((seq, HEAD_DIM - ROT_DIM))
    zeros = np.zeros((seq, HEAD_DIM - ROT_DIM))
    cos_t = np.concatenate([cos, cos, ones], axis=1)
    sin_t = np.concatenate([-sin, sin, zeros], axis=1)
    return jnp.asarray(cos_t, F32), jnp.asarray(sin_t, F32)


def _block_diag_gates(w_a, w_x, cw):
    nblk, bd, _ = w_a.shape
    per = cw // bd
    nt = nblk // per
    eye = jnp.eye(per, dtype=w_a.dtype)

    def bdiag(w):
        w = w.reshape(nt, per, bd, bd)
        full = jnp.einsum("tpij,pq->tpiqj", w, eye)
        return full.reshape(nt, cw, cw)

    return jnp.concatenate([bdiag(w_a), bdiag(w_x)], axis=2).astype(BF16)


def _in_proj(x2d, g, w_in, conv_w, conv_b, w_a, b_a, w_x, b_x, lam, seq, attn_width, rec_width,
             *, tm=512, tn=512, gate_tile=128):
    t, d = x2d.shape
    cols = w_in.shape[1]
    s_tiles = seq // tm
    cos_t, sin_t = _rope_tables(seq)
    wg = _block_diag_gates(w_a, w_x, gate_tile)
    kern = functools.partial(_in_proj_kernel, tn=tn, attn_width=attn_width, rec_width=rec_width,
                             gate_tile=gate_tile, s_tiles=s_tiles)
    whole = lambda shape: pl.BlockSpec(shape, lambda i: (0,) * len(shape))
    rec_row = lambda v: v.reshape(1, rec_width)
    rec_out = pl.BlockSpec((tm, rec_width), lambda i: (i, 0))
    rec_shape = jax.ShapeDtypeStruct((t, rec_width), F32)
    return pl.pallas_call(
        kern,
        grid=(t // tm,),
        in_specs=[
            pl.BlockSpec((tm, d), lambda i: (i, 0)),
            whole((1, d)),
            pl.BlockSpec((d, cols), lambda i: (0, 0), pipeline_mode=pl.Buffered(1)),
            pl.BlockSpec((tm, HEAD_DIM), lambda i: (i % s_tiles, 0)),
            pl.BlockSpec((tm, HEAD_DIM), lambda i: (i % s_tiles, 0)),
            whole((CONV_WIDTH, rec_width)),
            whole((1, rec_width)),
            whole(wg.shape),
            whole((1, rec_width)), whole((1, rec_width)), whole((1, rec_width)),
        ],
        out_specs=[pl.BlockSpec((tm, 3 * attn_width), lambda i: (i, 0)), rec_out, rec_out, rec_out],
        out_shape=[jax.ShapeDtypeStruct((t, 3 * attn_width), BF16), rec_shape, rec_shape, rec_shape],
        scratch_shapes=[pltpu.VMEM((tm + SUBLANES, rec_width), F32)],
        compiler_params=pltpu.CompilerParams(
            dimension_semantics=("arbitrary",), vmem_limit_bytes=VMEM_LIMIT),
        name="in_proj",
    )(x2d, g.reshape(1, d), w_in.astype(BF16), cos_t, sin_t, conv_w, rec_row(conv_b), wg,
      rec_row(b_a), rec_row(b_x), rec_row(lam))


def _moba_kernel(q_ref, k_ref, v_ref, o_ref, vt_scr, kmean_scr, s_scr, p_scr, *, nb):
    blk = MOBA_BLOCK
    scale_l2 = (HEAD_DIM ** -0.5) * LOG2E
    tb = (((1,), (1,)), ((), ()))

    ones_row = lax.broadcasted_iota(jnp.int32, (vt_scr.shape[0] - HEAD_DIM, vt_scr.shape[1]), 0) == 0
    vt_scr[HEAD_DIM:, :] = jnp.where(ones_row, 1.0, 0.0).astype(BF16)
    for n in range(nb):
        rows = slice(n * blk, (n + 1) * blk)
        vt_scr[:HEAD_DIM, rows] = v_ref[rows, :].astype(F32).T.astype(BF16)
        kmean_scr[n:n + 1, :] = jnp.mean(k_ref[rows, :].astype(F32), axis=0, keepdims=True)

    km = kmean_scr[...]
    km_hi = km.astype(BF16)
    km_lo = (km - km_hi.astype(F32)).astype(BF16)
    blk_id = lax.broadcasted_iota(jnp.int32, (nb, blk), 0)
    key_pos = lax.broadcasted_iota(jnp.int32, (blk, blk), 0)
    qry_pos = lax.broadcasted_iota(jnp.int32, (blk, blk), 1)
    causal = key_pos <= qry_pos

    def scores(qi):
        slot = qi % 2
        q = q_ref[qi * blk:(qi + 1) * blk, :]

        if qi <= MOBA_TOPK:
            bias = None
        else:
            gate = (lax.dot_general(km_hi, q, tb, preferred_element_type=F32)
                    + lax.dot_general(km_lo, q, tb, preferred_element_type=F32))
            eligible = blk_id < qi
            rem = jnp.where(eligible, gate, -jnp.inf)
            sel = jnp.zeros((nb, blk), jnp.bool_)
            for _t in range(MOBA_TOPK):
                mx = jnp.max(rem, axis=0, keepdims=True)
                first = jnp.min(jnp.where(rem == mx, blk_id, nb), axis=0, keepdims=True)
                pick = blk_id == first
                sel = jnp.logical_or(sel, pick)
                rem = jnp.where(pick, -jnp.inf, rem)
            bias = jnp.where(jnp.logical_and(sel, eligible), 0.0, NEG_BIG)

        m = None
        for n in range(qi + 1):
            rows = slice(n * blk, (n + 1) * blk)
            raw = lax.dot_general(k_ref[rows, :], q, tb, preferred_element_type=F32)
            if n == qi:
                raw = jnp.where(causal, raw, NEG_BIG)
            s_scr[slot, rows, :] = raw
            cm = jnp.max(raw, axis=0, keepdims=True) * scale_l2
            if bias is not None and n < qi:
                cm = cm + bias[n:n + 1, :]
            m = cm if m is None else jnp.maximum(m, cm)
        return bias, m

    def probabilities(qi, bias, m):
        slot = qi % 2
        pace = None
        for n in range(qi + 1):
            shift = -m
            if bias is not None and n < qi:
                shift = bias[n:n + 1, :] - m
            for r0 in range(n * blk, (n + 1) * blk, PACE_ROWS):
                rows = slice(r0, r0 + PACE_ROWS)
                sh = shift if pace is None else shift + pace
                x = s_scr[slot, rows, :] * scale_l2 + sh
                p = jnp.exp2(x.astype(BF16))
                p_scr[slot, rows, :] = p
                last = p[PACE_ROWS - 2 * SUBLANES:, :].astype(F32)
                pace = jnp.where(jnp.min(last, axis=0, keepdims=True) < 0.0, 1.0, 0.0)

    def weighted_values(qi):
        slot = qi % 2
        nk = (qi + 1) * blk
        acc = jnp.dot(vt_scr[:, 0:nk], p_scr[slot, 0:nk, :], preferred_element_type=F32)
        l = acc[HEAD_DIM:HEAD_DIM + 1, :]
        o_ref[qi * blk:(qi + 1) * blk, :] = (acc[:HEAD_DIM, :] / l).T

    state = scores(0)
    for qi in range(nb):
        probabilities(qi, *state)
        state = scores(qi + 1) if qi + 1 < nb else None
        weighted_values(qi)


def _moba(qkv, batch, seq):
    t = qkv.shape[0]
    nb = seq // MOBA_BLOCK
    nh = N_ATTN_HEADS
    return pl.pallas_call(
        functools.partial(_moba_kernel, nb=nb),
        grid=(batch, nh),
        in_specs=[
            pl.BlockSpec((seq, HEAD_DIM), lambda b, h: (b, h)),
            pl.BlockSpec((seq, HEAD_DIM), lambda b, h: (b, nh + h)),
            pl.BlockSpec((seq, HEAD_DIM), lambda b, h: (b, 2 * nh + h)),
        ],
        out_specs=pl.BlockSpec((seq, HEAD_DIM), lambda b, h: (b, h)),
        out_shape=jax.ShapeDtypeStruct((t, nh * HEAD_DIM), F32),
        scratch_shapes=[
            pltpu.VMEM((HEAD_DIM + 2 * SUBLANES, seq), BF16),
            pltpu.VMEM((nb, HEAD_DIM), F32),
            pltpu.VMEM((2, seq, MOBA_BLOCK), F32),
            pltpu.VMEM((2, seq, MOBA_BLOCK), BF16),
        ],
        compiler_params=pltpu.CompilerParams(
            dimension_semantics=("parallel", "parallel"), vmem_limit_bytes=VMEM_LIMIT),
        name="moba",
    )(qkv, qkv, qkv)


def _out_proj_kernel(x_ref, attn_ref, a_ref, b_ref, gg_ref, ga_ref, gr_ref, w_ref, pg_ref, o_ref, hcar,
                     *, s_tiles):
    i = pl.program_id(0)
    tm = x_ref.shape[0]

    @pl.when(i % s_tiles == 0)
    def _():
        hcar[...] = jnp.zeros_like(hcar)

    a = a_ref[...]
    b = b_ref[...]
    row8 = lax.broadcasted_iota(jnp.int32, a.shape, 0) & (SUBLANES - 1)
    for s in (1, 2, 4):
        ar = pltpu.roll(a, s, 0)
        br = pltpu.roll(b, s, 0)
        keep = row8 >= s
        b = jnp.where(keep, a * br + b, b)
        a = jnp.where(keep, a * ar, a)
    carry = hcar[...]
    groups = []
    for r0 in range(0, tm, SUBLANES):
        hg = a[r0:r0 + SUBLANES, :] * carry + b[r0:r0 + SUBLANES, :]
        groups.append(hg)
        carry = hg[SUBLANES - 1:SUBLANES, :]
    hcar[...] = carry
    rec = jnp.concatenate(groups, axis=0) * gg_ref[...]

    at = _rms(attn_ref[...], ga_ref[...]).astype(BF16)
    rc = _rms(rec, gr_ref[...]).astype(BF16)
    aw = at.shape[1]
    y = (jnp.dot(at, w_ref[:aw, :], preferred_element_type=F32)
         + jnp.dot(rc, w_ref[aw:, :], preferred_element_type=F32))
    o_ref[...] = x_ref[...] + _rms(y, pg_ref[...])


def _out_proj(x2d, attn, a, b, gg, ga, gr, w_out, pg, seq, *, tm=512):
    t, d = x2d.shape
    aw = attn.shape[1]
    rw = a.shape[1]
    rec_blk = pl.BlockSpec((tm, rw), lambda i: (i, 0))
    return pl.pallas_call(
        functools.partial(_out_proj_kernel, s_tiles=seq // tm),
        grid=(t // tm,),
        in_specs=[
            pl.BlockSpec((tm, d), lambda i: (i, 0)),
            pl.BlockSpec((tm, aw), lambda i: (i, 0)),
            rec_blk, rec_blk, rec_blk,
            pl.BlockSpec((1, aw), lambda i: (0, 0)),
            pl.BlockSpec((1, rw), lambda i: (0, 0)),
            pl.BlockSpec((aw + rw, d), lambda i: (0, 0), pipeline_mode=pl.Buffered(1)),
            pl.BlockSpec((1, d), lambda i: (0, 0)),
        ],
        out_specs=pl.BlockSpec((tm, d), lambda i: (i, 0)),
        out_shape=jax.ShapeDtypeStruct((t, d), F32),
        scratch_shapes=[pltpu.VMEM((1, rw), F32)],
        compiler_params=pltpu.CompilerParams(
            dimension_semantics=("arbitrary",), vmem_limit_bytes=VMEM_LIMIT),
        name="out_proj",
    )(x2d, attn, a, b, gg, ga.reshape(1, aw), gr.reshape(1, rw), w_out.astype(BF16), pg.reshape(1, d))


def kernel(x, ffn1_pre_g, ffn1_w_gate_up, ffn1_w_down, ffn1_post_g, mix_pre_g, w_in, conv_w, conv_b, rg_w_a, rg_b_a, rg_w_x, rg_b_x, rg_lambda, attn_out_g, rec_out_g, w_out, mix_post_g, ffn2_pre_g, ffn2_w_gate_up, ffn2_w_down, ffn2_post_g):
    batch, seq, d = x.shape
    depth = ffn1_pre_g.shape[0]
    attn_width = N_ATTN_HEADS * HEAD_DIM
    rec_width = conv_w.shape[-1]
    assert w_in.shape[-1] == 3 * attn_width + 2 * rec_width
    assert seq % 512 == 0 and seq % MOBA_BLOCK == 0
    h = x.reshape(batch * seq, d)
    for l in range(depth):
        h = _ffn(h, ffn1_pre_g[l], ffn1_w_gate_up[l], ffn1_w_down[l], ffn1_post_g[l])
        qkv, a, b, gg = _in_proj(h, mix_pre_g[l], w_in[l], conv_w[l], conv_b[l], rg_w_a[l], rg_b_a[l],
                                 rg_w_x[l], rg_b_x[l], rg_lambda[l], seq, attn_width, rec_width)
        attn = _moba(qkv, batch, seq)
        h = _out_proj(h, attn, a, b, gg, attn_out_g[l], rec_out_g[l], w_out[l], mix_post_g[l], seq)
        h = _ffn(h, ffn2_pre_g[l], ffn2_w_gate_up[l], ffn2_w_down[l], ffn2_post_g[l])
    return h.reshape(batch, seq, d)
```

```python
import functools

import jax
import jax.numpy as jnp
import numpy as np
from jax import lax
from jax.experimental import pallas as pl
from jax.experimental.pallas import tpu as pltpu

F32 = jnp.float32
BF16 = jnp.bfloat16

N_ATTN_HEADS = 8
HEAD_DIM = 128
ROT_DIM = 32
ROPE_THETA = 500000.0
MOBA_BLOCK = 256
MOBA_TOPK = 3
CONV_WIDTH = 4
RGLRU_C = 8.0
FFN_HALF = 0.5
EPS = 1e-6

VMEM_LIMIT = 56 * 1024 * 1024
LANES = 128
SUBLANES = 8

NEG_BIG = -1e30
PACE_ROWS = 128
LOG2E = 1.4426950408889634


def _rms(x, g):
    ms = jnp.mean(x * x, axis=-1, keepdims=True)
    return x * lax.rsqrt(ms + EPS) * g


def _ffn_steps(j, last, x_ref, preg_ref, postg_ref, o_ref, h_scr, weights, row_blk):
    row_blocks = [slice(r, r + row_blk) for r in range(0, x_ref.shape[0], row_blk)]

    def swiglu_down(h, w):
        wg, wu, wd = w
        g = jnp.dot(h, wg, preferred_element_type=F32)
        u = jnp.dot(h, wu, preferred_element_type=F32)
        a = (g * jax.nn.sigmoid(g) * u).astype(BF16)
        return jnp.dot(a, wd, preferred_element_type=F32)

    @pl.when(j == 0)
    def _():
        w = weights(False)
        for rows in row_blocks:
            h = _rms(x_ref[rows, :], preg_ref[...]).astype(BF16)
            h_scr[rows, :] = h
            o_ref[rows, :] = swiglu_down(h, w)

    @pl.when(jnp.logical_and(j > 0, j < last))
    def _():
        wg, wu, wd = weights(False)
        h = h_scr[...]
        n_split = 2 if wg.shape[1] % (4 * LANES) == 0 else 1
        part = wg.shape[1] // n_split
        for c0 in range(0, wg.shape[1], part):
            o_ref[...] += swiglu_down(h, (wg[:, c0:c0 + part], wu[:, c0:c0 + part], wd[c0:c0 + part, :]))

    @pl.when(j == last)
    def _():
        w = weights(True)
        paces = []
        for r, rows in enumerate(row_blocks):
            h = h_scr[rows, :]
            if r >= 2:
                h = h + paces[r - 2]
            acc = o_ref[rows, :] + swiglu_down(h, w)
            out = x_ref[rows, :] + FFN_HALF * _rms(acc, postg_ref[...])
            o_ref[rows, :] = out
            tail = jnp.max(out[row_blk - SUBLANES:, :], axis=0, keepdims=True)
            paces.append(jnp.where(tail > jnp.inf, 1.0, 0.0).astype(BF16))


def _ffn_head_kernel(x_ref, preg_ref, wg_ref, wu0_ref, wu1_ref, wd_ref, postg_ref,
                     o_ref, cg_ref, cu_ref, cd_ref, h_scr, *, n_last, row_blk):
    j = pl.program_id(0)

    def weights(is_last):
        n = n_last if is_last else 2
        keep = n * LANES
        pad = (2 - n) * LANES
        wg = wg_ref[:, :keep].astype(BF16)
        wd = wd_ref[:keep, :].astype(BF16)
        wu = jnp.concatenate([r[...].astype(BF16) for r in (wu0_ref, wu1_ref)[:n]], axis=1)
        if pad:
            zc = jnp.zeros((wg.shape[0], pad), BF16)
            wg = jnp.concatenate([wg, zc], axis=1)
            wu = jnp.concatenate([wu, zc], axis=1)
            wd = jnp.concatenate([wd, jnp.zeros((pad, wd.shape[1]), BF16)], axis=0)
        w = wg, wu, wd
        cg_ref[...], cu_ref[...], cd_ref[...] = w
        return w

    _ffn_steps(j, pl.num_programs(0) - 1, x_ref, preg_ref, postg_ref, o_ref, h_scr, weights, row_blk)


def _ffn_tail_kernel(x_ref, preg_ref, wg_ref, wu_ref, wd_ref, postg_ref, head_out_ref, o_ref, h_scr,
                     *, row_blk):
    del head_out_ref
    weights = lambda is_last: (wg_ref[...], wu_ref[...], wd_ref[...])
    _ffn_steps(pl.program_id(1), pl.num_programs(1) - 1, x_ref, preg_ref, postg_ref, o_ref, h_scr,
               weights, row_blk)


def _ffn(x2d, pre_g, w_gate_up, w_down, post_g, *, tm=1024, tf=512, row_blk=256):
    t, d = x2d.shape
    f = w_down.shape[0]
    assert f % LANES == 0 and tf % (2 * LANES) == 0
    nblk = f // LANES
    fp = -(-f // tf) * tf
    n_head = fp // (2 * LANES)
    n_last = nblk - 2 * (n_head - 1)
    assert 1 <= n_last <= 2
    pre = pre_g.reshape(1, d)
    post = post_g.reshape(1, d)
    vec = pl.BlockSpec((1, d), lambda *idx: (0, 0))
    col = lambda k, base: pl.BlockSpec(
        (d, LANES), lambda j: (0, base + jnp.minimum(2 * j + k, nblk - 1)))
    once = pl.Buffered(1)
    head_out, cg, cu, cd = pl.pallas_call(
        functools.partial(_ffn_head_kernel, n_last=n_last, row_blk=row_blk),
        grid=(n_head,),
        in_specs=[
            pl.BlockSpec((tm, d), lambda j: (0, 0), pipeline_mode=once),
            vec,
            pl.BlockSpec((d, 2 * LANES), lambda j: (0, j)),
            col(0, nblk), col(1, nblk),
            pl.BlockSpec((2 * LANES, d), lambda j: (j, 0)),
            vec,
        ],
        out_specs=[
            pl.BlockSpec((tm, d), lambda j: (0, 0)),
            pl.BlockSpec((d, 2 * LANES), lambda j: (0, j)),
            pl.BlockSpec((d, 2 * LANES), lambda j: (0, j)),
            pl.BlockSpec((2 * LANES, d), lambda j: (j, 0)),
        ],
        out_shape=[
            jax.ShapeDtypeStruct((t, d), F32),
            jax.ShapeDtypeStruct((d, fp), BF16),
            jax.ShapeDtypeStruct((d, fp), BF16),
            jax.ShapeDtypeStruct((fp, d), BF16),
        ],
        scratch_shapes=[pltpu.VMEM((tm, d), BF16)],
        compiler_params=pltpu.CompilerParams(
            dimension_semantics=("arbitrary",), vmem_limit_bytes=VMEM_LIMIT),
        name="ffn_head",
    )(x2d, pre, w_gate_up, w_gate_up, w_gate_up, w_down, post)
    if t == tm:
        return head_out
    return pl.pallas_call(
        functools.partial(_ffn_tail_kernel, row_blk=row_blk),
        grid=(t // tm - 1, fp // tf),
        in_specs=[
            pl.BlockSpec((tm, d), lambda i, j: (i + 1, 0)),
            vec,
            pl.BlockSpec((d, tf), lambda i, j: (0, j)),
            pl.BlockSpec((d, tf), lambda i, j: (0, j)),
            pl.BlockSpec((tf, d), lambda i, j: (j, 0)),
            vec,
            pl.BlockSpec(memory_space=pl.ANY),
        ],
        out_specs=pl.BlockSpec((tm, d), lambda i, j: (i + 1, 0)),
        out_shape=jax.ShapeDtypeStruct((t, d), F32),
        input_output_aliases={6: 0},
        scratch_shapes=[pltpu.VMEM((tm, d), BF16)],
        compiler_params=pltpu.CompilerParams(
            dimension_semantics=("parallel", "arbitrary"), vmem_limit_bytes=VMEM_LIMIT),
        name="ffn_tail",
    )(x2d, pre, cg, cu, cd, post, head_out)


def _in_proj_kernel(x_ref, g_ref, w_ref, cos_ref, sin_ref, cw_ref, cb_ref, wg_ref, ba_ref, bx_ref,
                    lam_ref, qkv_ref, a_ref, b_ref, gg_ref, xbuf,
                    *, tn, attn_width, rec_width, gate_tile, s_tiles):
    i = pl.program_id(0)
    tm = x_ref.shape[0]
    pad = SUBLANES

    @pl.when(i % s_tiles == 0)
    def _():
        xbuf[0:pad, :] = jnp.zeros((pad, rec_width), F32)

    h = _rms(x_ref[...], g_ref[...]).astype(BF16)
    c = cos_ref[...]
    s = sin_ref[...]
    lane = lax.broadcasted_iota(jnp.int32, c.shape, 1)
    first_half = lane < ROT_DIM // 2

    def proj(col0):
        return jnp.dot(h, w_ref[:, col0:col0 + tn], preferred_element_type=F32)

    def emit_qk(col0):
        y = proj(col0)
        for hh in range(tn // HEAD_DIM):
            yh = y[:, hh * HEAD_DIM:(hh + 1) * HEAD_DIM]
            partner = jnp.where(first_half,
                                pltpu.roll(yh, HEAD_DIM - ROT_DIM // 2, 1),
                                pltpu.roll(yh, ROT_DIM // 2, 1))
            col = col0 + hh * HEAD_DIM
            qkv_ref[:, col:col + HEAD_DIM] = (yh * c + partner * s).astype(BF16)

    def emit_v(col0):
        qkv_ref[:, col0:col0 + tn] = proj(col0).astype(BF16)

    def emit_gate_gelu(col0):
        gg_ref[:, col0:col0 + tn] = jax.nn.gelu(proj(3 * attn_width + rec_width + col0))

    w = cw_ref[...]

    def emit_recurrence_inputs(ct):
        cols = slice(ct * gate_tile, (ct + 1) * gate_tile)
        xc = cb_ref[:, cols] + w[CONV_WIDTH - 1:CONV_WIDTH, cols] * xbuf[pad:pad + tm, cols]
        for j in range(1, CONV_WIDTH):
            xc = xc + w[CONV_WIDTH - 1 - j:CONV_WIDTH - j, cols] * xbuf[pad - j:pad - j + tm, cols]
        gates = jnp.dot(xc.astype(BF16), wg_ref[ct], preferred_element_type=F32)
        r = jax.nn.sigmoid(gates[:, :gate_tile] + ba_ref[:, cols])
        gi = jax.nn.sigmoid(gates[:, gate_tile:] + bx_ref[:, cols])
        z = -lam_ref[:, cols]
        softplus = jnp.maximum(z, 0.0) + jnp.log1p(jnp.exp(-jnp.abs(z)))
        log_a = (-RGLRU_C * r) * softplus
        a = jnp.exp(log_a)
        a_ref[:, cols] = a
        b_ref[:, cols] = jnp.sqrt(-jnp.tanh(log_a) * (a * a + 1.0)) * (gi * xc)

    for col0 in range(0, rec_width, tn):
        xbuf[pad:pad + tm, col0:col0 + tn] = proj(3 * attn_width + col0)
    matmul_work = ([functools.partial(emit_qk, c0) for c0 in range(0, 2 * attn_width, tn)]
                   + [functools.partial(emit_v, c0) for c0 in range(2 * attn_width, 3 * attn_width, tn)]
                   + [functools.partial(emit_gate_gelu, c0) for c0 in range(0, rec_width, tn)])
    n_gate_tiles = rec_width // gate_tile
    per = -(-len(matmul_work) // n_gate_tiles)
    for ct in range(n_gate_tiles):
        emit_recurrence_inputs(ct)
        for work in matmul_work[ct * per:(ct + 1) * per]:
            work()
    for work in matmul_work[n_gate_tiles * per:]:
        work()
    xbuf[0:pad, :] = xbuf[tm:tm + pad, :]


def _rope_tables(seq):
    inv_freq = ROPE_THETA ** (-np.arange(0, ROT_DIM, 2, dtype=np.float64) / ROT_DIM)
    ang = np.arange(seq, dtype=np.float64)[:, None] * inv_freq[None, :]
    cos, sin = np.cos(ang), np.sin(ang)
    ones = np.ones((seq, HEAD_DIM - ROT_DIM))
    zeros = np.zeros((seq, HEAD_DIM - ROT_DIM))
    cos_t = np.concatenate([cos, cos, ones], axis=1)
    sin_t = np.concatenate([-sin, sin, zeros], axis=1)
    return jnp.asarray(cos_t, F32), jnp.asarray(sin_t, F32)


def _block_diag_gates(w_a, w_x, cw):
    nblk, bd, _ = w_a.shape
    per = cw // bd
    nt = nblk // per
    eye = jnp.eye(per, dtype=w_a.dtype)

    def bdiag(w):
        w = w.reshape(nt, per, bd, bd)
        full = jnp.einsum("tpij,pq->tpiqj", w, eye)
        return full.reshape(nt, cw, cw)

    return jnp.concatenate([bdiag(w_a), bdiag(w_x)], axis=2).astype(BF16)


def _in_proj(x2d, g, w_in, conv_w, conv_b, w_a, b_a, w_x, b_x, lam, seq, attn_width, rec_width,
             *, tm=512, tn=512, gate_tile=128):
    t, d = x2d.shape
    cols = w_in.shape[1]
    s_tiles = seq // tm
    cos_t, sin_t = _rope_tables(seq)
    wg = _block_diag_gates(w_a, w_x, gate_tile)
    kern = functools.partial(_in_proj_kernel, tn=tn, attn_width=attn_width, rec_width=rec_width,
                             gate_tile=gate_tile, s_tiles=s_tiles)
    whole = lambda shape: pl.BlockSpec(shape, lambda i: (0,) * len(shape))
    rec_row = lambda v: v.reshape(1, rec_width)
    rec_out = pl.BlockSpec((tm, rec_width), lambda i: (i, 0))
    rec_shape = jax.ShapeDtypeStruct((t, rec_width), F32)
    return pl.pallas_call(
        kern,
        grid=(t // tm,),
        in_specs=[
            pl.BlockSpec((tm, d), lambda i: (i, 0)),
            whole((1, d)),
            pl.BlockSpec((d, cols), lambda i: (0, 0), pipeline_mode=pl.Buffered(1)),
            pl.BlockSpec((tm, HEAD_DIM), lambda i: (i % s_tiles, 0)),
            pl.BlockSpec((tm, HEAD_DIM), lambda i: (i % s_tiles, 0)),
            whole((CONV_WIDTH, rec_width)),
            whole((1, rec_width)),
            whole(wg.shape),
            whole((1, rec_width)), whole((1, rec_width)), whole((1, rec_width)),
        ],
        out_specs=[pl.BlockSpec((tm, 3 * attn_width), lambda i: (i, 0)), rec_out, rec_out, rec_out],
        out_shape=[jax.ShapeDtypeStruct((t, 3 * attn_width), BF16), rec_shape, rec_shape, rec_shape],
        scratch_shapes=[pltpu.VMEM((tm + SUBLANES, rec_width), F32)],
        compiler_params=pltpu.CompilerParams(
            dimension_semantics=("arbitrary",), vmem_limit_bytes=VMEM_LIMIT),
        name="in_proj",
    )(x2d, g.reshape(1, d), w_in.astype(BF16), cos_t, sin_t, conv_w, rec_row(conv_b), wg,
      rec_row(b_a), rec_row(b_x), rec_row(lam))


def _moba_kernel(q_ref, k_ref, v_ref, o_ref, vt_scr, kmean_scr, s_scr, p_scr, *, nb):
    blk = MOBA_BLOCK
    scale_l2 = (HEAD_DIM ** -0.5) * LOG2E
    tb = (((1,), (1,)), ((), ()))

    ones_row = lax.broadcasted_iota(jnp.int32, (vt_scr.shape[0] - HEAD_DIM, vt_scr.shape[1]), 0) == 0
    vt_scr[HEAD_DIM:, :] = jnp.where(ones_row, 1.0, 0.0).astype(BF16)
    for n in range(nb):
        rows = slice(n * blk, (n + 1) * blk)
        vt_scr[:HEAD_DIM, rows] = v_ref[rows, :].astype(F32).T.astype(BF16)
        kmean_scr[n:n + 1, :] = jnp.mean(k_ref[rows, :].astype(F32), axis=0, keepdims=True)

    km = kmean_scr[...]
    km_hi = km.astype(BF16)
    km_lo = (km - km_hi.astype(F32)).astype(BF16)
    blk_id = lax.broadcasted_iota(jnp.int32, (nb, blk), 0)
    key_pos = lax.broadcasted_iota(jnp.int32, (blk, blk), 0)
    qry_pos = lax.broadcasted_iota(jnp.int32, (blk, blk), 1)
    causal = key_pos <= qry_pos

    def scores(qi):
        slot = qi % 2
        q = q_ref[qi * blk:(qi + 1) * blk, :]

        if qi <= MOBA_TOPK:
            bias = None
        else:
            gate = (lax.dot_general(km_hi, q, tb, preferred_element_type=F32)
                    + lax.dot_general(km_lo, q, tb, preferred_element_type=F32))
            eligible = blk_id < qi
            rem = jnp.where(eligible, gate, -jnp.inf)
            sel = jnp.zeros((nb, blk), jnp.bool_)
            for _t in range(MOBA_TOPK):
                mx = jnp.max(rem, axis=0, keepdims=True)
                first = jnp.min(jnp.where(rem == mx, blk_id, nb), axis=0, keepdims=True)
                pick = blk_id == first
                sel = jnp.logical_or(sel, pick)
                rem = jnp.where(pick, -jnp.inf, rem)
            bias = jnp.where(jnp.logical_and(sel, eligible), 0.0, NEG_BIG)

        m = None
        for n in range(qi + 1):
            rows = slice(n * blk, (n + 1) * blk)
            raw = lax.dot_general(k_ref[rows, :], q, tb, preferred_element_type=F32)
            if n == qi:
                raw = jnp.where(causal, raw, NEG_BIG)
            s_scr[slot, rows, :] = raw
            cm = jnp.max(raw, axis=0, keepdims=True) * scale_l2
            if bias is not None and n < qi:
                cm = cm + bias[n:n + 1, :]
            m = cm if m is None else jnp.maximum(m, cm)
        return bias, m

    def probabilities(qi, bias, m):
        slot = qi % 2
        pace = None
        for n in range(qi + 1):
            shift = -m
            if bias is not None and n < qi:
                shift = bias[n:n + 1, :] - m
            for r0 in range(n * blk, (n + 1) * blk, PACE_ROWS):
                rows = slice(r0, r0 + PACE_ROWS)
                sh = shift if pace is None else shift + pace
                x = s_scr[slot, rows, :] * scale_l2 + sh
                p = jnp.exp2(x.astype(BF16))
                p_scr[slot, rows, :] = p
                last = p[PACE_ROWS - 2 * SUBLANES:, :].astype(F32)
                pace = jnp.where(jnp.min(last, axis=0, keepdims=True) < 0.0, 1.0, 0.0)

    def weighted_values(qi):
        slot = qi % 2
        nk = (qi + 1) * blk
        acc = jnp.dot(vt_scr[:, 0:nk], p_scr[slot, 0:nk, :], preferred_element_type=F32)
        l = acc[HEAD_DIM:HEAD_DIM + 1, :]
        o_ref[qi * blk:(qi + 1) * blk, :] = (acc[:HEAD_DIM, :] / l).T

    state = scores(0)
    for qi in range(nb):
        probabilities(qi, *state)
        state = scores(qi + 1) if qi + 1 < nb else None
        weighted_values(qi)


def _moba(qkv, batch, seq):
    t = qkv.shape[0]
    nb = seq // MOBA_BLOCK
    nh = N_ATTN_HEADS
    return pl.pallas_call(
        functools.partial(_moba_kernel, nb=nb),
        grid=(batch, nh),
        in_specs=[
            pl.BlockSpec((seq, HEAD_DIM), lambda b, h: (b, h)),
            pl.BlockSpec((seq, HEAD_DIM), lambda b, h: (b, nh + h)),
            pl.BlockSpec((seq, HEAD_DIM), lambda b, h: (b, 2 * nh + h)),
        ],
        out_specs=pl.BlockSpec((seq, HEAD_DIM), lambda b, h: (b, h)),
        out_shape=jax.ShapeDtypeStruct((t, nh * HEAD_DIM), F32),
        scratch_shapes=[
            pltpu.VMEM((HEAD_DIM + 2 * SUBLANES, seq), BF16),
            pltpu.VMEM((nb, HEAD_DIM), F32),
            pltpu.VMEM((2, seq, MOBA_BLOCK), F32),
            pltpu.VMEM((2, seq, MOBA_BLOCK), BF16),
        ],
        compiler_params=pltpu.CompilerParams(
            dimension_semantics=("parallel", "parallel"), vmem_limit_bytes=VMEM_LIMIT),
        name="moba",
    )(qkv, qkv, qkv)


def _out_proj_kernel(x_ref, attn_ref, a_ref, b_ref, gg_ref, ga_ref, gr_ref, w_ref, pg_ref, o_ref, hcar,
                     *, s_tiles):
    i = pl.program_id(0)
    tm = x_ref.shape[0]

    @pl.when(i % s_tiles == 0)
    def _():
        hcar[...] = jnp.zeros_like(hcar)

    a = a_ref[...]
    b = b_ref[...]
    row8 = lax.broadcasted_iota(jnp.int32, a.shape, 0) & (SUBLANES - 1)
    for s in (1, 2, 4):
        ar = pltpu.roll(a, s, 0)
        br = pltpu.roll(b, s, 0)
        keep = row8 >= s
        b = jnp.where(keep, a * br + b, b)
        a = jnp.where(keep, a * ar, a)
    carry = hcar[...]
    groups = []
    for r0 in range(0, tm, SUBLANES):
        hg = a[r0:r0 + SUBLANES, :] * carry + b[r0:r0 + SUBLANES, :]
        groups.append(hg)
        carry = hg[SUBLANES - 1:SUBLANES, :]
    hcar[...] = carry
    rec = jnp.concatenate(groups, axis=0) * gg_ref[...]

    at = _rms(attn_ref[...], ga_ref[...]).astype(BF16)
    rc = _rms(rec, gr_ref[...]).astype(BF16)
    aw = at.shape[1]
    y = (jnp.dot(at, w_ref[:aw, :], preferred_element_type=F32)
         + jnp.dot(rc, w_ref[aw:, :], preferred_element_type=F32))
    o_ref[...] = x_ref[...] + _rms(y, pg_ref[...])


def _out_proj(x2d, attn, a, b, gg, ga, gr, w_out, pg, seq, *, tm=512):
    t, d = x2d.shape
    aw = attn.shape[1]
    rw = a.shape[1]
    rec_blk = pl.BlockSpec((tm, rw), lambda i: (i, 0))
    return pl.pallas_call(
        functools.partial(_out_proj_kernel, s_tiles=seq // tm),
        grid=(t // tm,),
        in_specs=[
            pl.BlockSpec((tm, d), lambda i: (i, 0)),
            pl.BlockSpec((tm, aw), lambda i: (i, 0)),
            rec_blk, rec_blk, rec_blk,
            pl.BlockSpec((1, aw), lambda i: (0, 0)),
            pl.BlockSpec((1, rw), lambda i: (0, 0)),
            pl.BlockSpec((aw + rw, d), lambda i: (0, 0), pipeline_mode=pl.Buffered(1)),
            pl.BlockSpec((1, d), lambda i: (0, 0)),
        ],
        out_specs=pl.BlockSpec((tm, d), lambda i: (i, 0)),
        out_shape=jax.ShapeDtypeStruct((t, d), F32),
        scratch_shapes=[pltpu.VMEM((1, rw), F32)],
        compiler_params=pltpu.CompilerParams(
            dimension_semantics=("arbitrary",), vmem_limit_bytes=VMEM_LIMIT),
        name="out_proj",
    )(x2d, attn, a, b, gg, ga.reshape(1, aw), gr.reshape(1, rw), w_out.astype(BF16), pg.reshape(1, d))


def kernel(x, ffn1_pre_g, ffn1_w_gate_up, ffn1_w_down, ffn1_post_g, mix_pre_g, w_in, conv_w, conv_b, rg_w_a, rg_b_a, rg_w_x, rg_b_x, rg_lambda, attn_out_g, rec_out_g, w_out, mix_post_g, ffn2_pre_g, ffn2_w_gate_up, ffn2_w_down, ffn2_post_g):
    batch, seq, d = x.shape
    depth = ffn1_pre_g.shape[0]
    attn_width = N_ATTN_HEADS * HEAD_DIM
    rec_width = conv_w.shape[-1]
    assert w_in.shape[-1] == 3 * attn_width + 2 * rec_width
    assert seq % 512 == 0 and seq % MOBA_BLOCK == 0
    h = x.reshape(batch * seq, d)
    for l in range(depth):
        h = _ffn(h, ffn1_pre_g[l], ffn1_w_gate_up[l], ffn1_w_down[l], ffn1_post_g[l])
        qkv, a, b, gg = _in_proj(h, mix_pre_g[l], w_in[l], conv_w[l], conv_b[l], rg_w_a[l], rg_b_a[l],
                                 rg_w_x[l], rg_b_x[l], rg_lambda[l], seq, attn_width, rec_width)
        attn = _moba(qkv, batch, seq)
        h = _out_proj(h, attn, a, b, gg, attn_out_g[l], rec_out_g[l], w_out[l], mix_post_g[l], seq)
        h = _ffn(h, ffn2_pre_g[l], ffn2_w_gate_up[l], ffn2_w_down[l], ffn2_post_g[l])
    return h.reshape(batch, seq, d)
```
